```python
import jax, jax.numpy as jnp
from jax import lax
import numpy as np

D_MODEL = 4096
BATCH = 4
SEQ = 2048
DEPTH = 1

MIX_WIDTH = D_MODEL
ATTN_HEAD_DIM = 128
ATTN_WIDTH = MIX_WIDTH // 2
ATTN_HEADS = ATTN_WIDTH // ATTN_HEAD_DIM
DILATED_BRANCHES = ((128, 1), (512, 4), (2048, 16))
ATTN_BLOCK = 128
ROPE_THETA = 10000.0
SSD_WIDTH = MIX_WIDTH - ATTN_WIDTH
SSD_HEAD_DIM = 64
SSD_HEADS = SSD_WIDTH // SSD_HEAD_DIM
SSD_GROUPS = 8
SSD_STATE = 128
SSD_CONV = 4
SSD_CHUNK = 256
SSD_CONV_DIM = SSD_WIDTH + 2 * SSD_GROUPS * SSD_STATE
IN_PROJ_WIDTH = 3 * ATTN_WIDTH + SSD_WIDTH + SSD_CONV_DIM + SSD_HEADS
IN_SPLITS = (ATTN_WIDTH, 2 * ATTN_WIDTH, 3 * ATTN_WIDTH,
             3 * ATTN_WIDTH + SSD_WIDTH, 3 * ATTN_WIDTH + SSD_WIDTH + SSD_CONV_DIM)
N_EXPERT_GROUPS = 4
EXPERTS_PER_GROUP = 8
N_EXPERTS = N_EXPERT_GROUPS * EXPERTS_PER_GROUP
TOP_K = 2
EXPERT_FF = D_MODEL // 4
MOE_BLOCK = 128
NORM_EPS = 1e-6
SSD_NORM_EPS = 1e-5

kernel_name = 'hymba_dilated_ssd_hmoe_block'


def rms_norm(x, w, eps):
    xf = x.astype(jnp.float32)
    y = xf * lax.rsqrt(jnp.mean(xf * xf, axis=-1, keepdims=True) + eps)
    return (y * w.astype(jnp.float32)).astype(x.dtype)


def rotary(t, positions):
    half = t.shape[-1] // 2
    inv_freq = jnp.power(jnp.float32(ROPE_THETA), -jnp.arange(half, dtype=jnp.float32) / half)
    ang = positions.astype(jnp.float32)[..., None] * inv_freq
    cos = jnp.cos(ang)[:, :, None, :]
    sin = jnp.sin(ang)[:, :, None, :]
    tf = t.astype(jnp.float32)
    t1, t2 = tf[..., :half], tf[..., half:]
    return jnp.concatenate([t1 * cos - t2 * sin, t2 * cos + t1 * sin], axis=-1).astype(t.dtype)


def dilated_branch(q, k, v, n_back, dil):
    b, s, h, dh = q.shape
    L = s // dil
    nb = -(-L // ATTN_BLOCK)
    lp = nb * ATTN_BLOCK

    def to_sub(t):
        t = t.reshape(b, L, dil, h, dh).transpose(0, 2, 3, 1, 4)
        t = jnp.pad(t, ((0, 0), (0, 0), (0, 0), (0, lp - L), (0, 0)))
        return t.reshape(b, dil, h, nb, ATTN_BLOCK, dh)

    def with_prev(t):
        prev = jnp.pad(t, ((0, 0), (0, 0), (0, 0), (1, 0), (0, 0), (0, 0)))[:, :, :, :-1]
        return jnp.concatenate([prev, t], axis=4)

    qs = to_sub(q)
    kk = with_prev(to_sub(k))
    vv = with_prev(to_sub(v))
    scale = jnp.float32(dh ** -0.5)
    sc = jnp.einsum('brhnqc,brhnkc->brhnqk', qs, kk).astype(jnp.float32) * scale
    qi = jnp.arange(ATTN_BLOCK)[:, None]
    kj = jnp.arange(2 * ATTN_BLOCK)[None, :]
    dist = ATTN_BLOCK + qi - kj
    kpos = (jnp.arange(nb)[:, None, None] - 1) * ATTN_BLOCK + kj[None]
    mask = (dist >= 0)[None] & (dist <= n_back)[None] & (kpos >= 0)
    sc = jnp.where(mask, sc, -jnp.inf)
    m = jnp.max(sc, axis=-1)
    p = jnp.exp(sc - m[..., None])
    l = jnp.sum(p, axis=-1)
    o = jnp.einsum('brhnqk,brhnkc->brhnqc', p, vv.astype(jnp.float32)) / l[..., None]
    o = o.reshape(b, dil, h, lp, dh)[:, :, :, :L].transpose(0, 3, 1, 2, 4).reshape(b, s, h, dh)
    back = lambda t: t.reshape(b, dil, h, lp)[..., :L].transpose(0, 3, 1, 2).reshape(b, s, h)
    return o, back(m), back(l)


def dilated_attention(q, k, v):
    outs, ms, ls = [], [], []
    for window, dil in DILATED_BRANCHES:
        o, m, l = dilated_branch(q, k, v, window // dil, dil)
        outs.append(o); ms.append(m); ls.append(l)
    m_all = jnp.stack(ms)
    wts = jnp.stack(ls) * jnp.exp(m_all - jnp.max(m_all, axis=0, keepdims=True))
    o = jnp.sum(wts[..., None] * jnp.stack(outs), axis=0) / jnp.sum(wts, axis=0)[..., None]
    return o.astype(q.dtype)


def causal_depthwise_conv(x, w, bias):
    c = x.shape[-1]
    y = lax.conv_general_dilated(x, w[:, None, :].astype(x.dtype), window_strides=(1,),
                                 padding=[(SSD_CONV - 1, 0)],
                                 dimension_numbers=('NWC', 'WIO', 'NWC'),
                                 feature_group_count=c)
    return y + bias.astype(x.dtype)


def ssd_chunked(x, dA, Bm, Cm):
    b, s, h, p = x.shape
    g, n = Bm.shape[2], Bm.shape[3]
    r = h // g
    nc = -(-s // SSD_CHUNK)
    sp = nc * SSD_CHUNK
    padseq = lambda t: jnp.pad(t, [(0, 0), (0, sp - s)] + [(0, 0)] * (t.ndim - 2))
    xc = padseq(x).reshape(b, nc, SSD_CHUNK, g, r, p)
    bc = padseq(Bm).reshape(b, nc, SSD_CHUNK, g, n)
    cc = padseq(Cm).reshape(b, nc, SSD_CHUNK, g, n)
    ac = padseq(dA).reshape(b, nc, SSD_CHUNK, g, r).transpose(0, 3, 4, 1, 2)
    cs = jnp.cumsum(ac, axis=-1)
    idx = jnp.arange(SSD_CHUNK)
    causal = idx[:, None] >= idx[None, :]
    seg = jnp.exp(jnp.where(causal, cs[..., :, None] - cs[..., None, :], -jnp.inf))
    cb = jnp.einsum('bclgn,bcsgn->bcgls', cc, bc)
    y_diag = jnp.einsum('bcgls,bgrcls,bcsgrp->bclgrp', cb, seg, xc)
    decay_states = jnp.exp(cs[..., -1:] - cs)
    states = jnp.einsum('bclgn,bgrcl,bclgrp->cbgrpn', bc, decay_states, xc)
    chunk_decay = jnp.exp(cs[..., -1]).transpose(3, 0, 1, 2)

    def step(h_prev, inp):
        dec, st = inp
        return dec[..., None, None] * h_prev + st, h_prev

    h0 = jnp.zeros((b, g, r, p, n), jnp.float32)
    _, prev = lax.scan(step, h0, (chunk_decay, states))
    y_off = jnp.einsum('bclgn,cbgrpn,bgrcl->bclgrp', cc, prev, jnp.exp(cs))
    return (y_diag + y_off).reshape(b, sp, h, p)[:, :s]


def ssd_mixer(z, xbc_raw, dt_raw, conv_w, conv_b, dt_bias, a_log, d_skip, norm_w):
    b, s, _ = z.shape
    xbc = jax.nn.silu(causal_depthwise_conv(xbc_raw, conv_w, conv_b))
    xs, bm, cm = jnp.split(xbc, [SSD_WIDTH, SSD_WIDTH + SSD_GROUPS * SSD_STATE], axis=-1)
    xs = xs.reshape(b, s, SSD_HEADS, SSD_HEAD_DIM).astype(jnp.float32)
    bm = bm.reshape(b, s, SSD_GROUPS, SSD_STATE).astype(jnp.float32)
    cm = cm.reshape(b, s, SSD_GROUPS, SSD_STATE).astype(jnp.float32)
    dt = jax.nn.softplus(dt_raw.astype(jnp.float32) + dt_bias.astype(jnp.float32))
    a = -jnp.exp(a_log.astype(jnp.float32))
    y = ssd_chunked(xs * dt[..., None], dt * a, bm, cm)
    y = y + d_skip.astype(jnp.float32)[:, None] * xs
    y = y.reshape(b, s, SSD_WIDTH) * jax.nn.silu(z.astype(jnp.float32))
    yg = y.reshape(b, s, SSD_GROUPS, SSD_WIDTH // SSD_GROUPS)
    yg = yg * lax.rsqrt(jnp.mean(yg * yg, axis=-1, keepdims=True) + SSD_NORM_EPS)
    y = yg.reshape(b, s, SSD_WIDTH) * norm_w.astype(jnp.float32)
    return y.astype(z.dtype)


def hierarchical_moe(h, router_group_w, router_group_b, router_expert_w, router_expert_b,
                     w_gate, w_up, w_down):
    t, d = h.shape
    hf = h.astype(jnp.float32)
    g_probs = jax.nn.softmax(hf @ router_group_w.astype(jnp.float32) + router_group_b.astype(jnp.float32), axis=-1)
    g_p, g_idx = lax.top_k(g_probs, 1)
    e_logits = (hf @ router_expert_w.astype(jnp.float32) + router_expert_b.astype(jnp.float32))
    e_logits = e_logits.reshape(t, N_EXPERT_GROUPS, EXPERTS_PER_GROUP)
    e_sel = jnp.take_along_axis(e_logits, g_idx[:, :, None], axis=1)[:, 0]
    e_p, e_local = lax.top_k(jax.nn.softmax(e_sel, axis=-1), TOP_K)
    e_p = e_p / jnp.sum(e_p, axis=-1, keepdims=True)
    gates = g_p * e_p
    expert_ids = g_idx * EXPERTS_PER_GROUP + e_local

    n_assign = t * TOP_K
    flat_e = expert_ids.reshape(n_assign)
    flat_tok = jnp.repeat(jnp.arange(t, dtype=jnp.int32), TOP_K)
    flat_gate = gates.reshape(n_assign)
    order = jnp.argsort(flat_e, stable=True)
    e_sorted, tok_sorted, gate_sorted = flat_e[order], flat_tok[order], flat_gate[order]
    counts = jnp.bincount(flat_e, length=N_EXPERTS)
    starts = jnp.cumsum(counts) - counts
    padded = ((counts + MOE_BLOCK - 1) // MOE_BLOCK) * MOE_BLOCK
    pstarts = jnp.cumsum(padded) - padded
    dest = pstarts[e_sorted] + (jnp.arange(n_assign) - starts[e_sorted])
    n_blocks = n_assign // MOE_BLOCK + N_EXPERTS
    rows = n_blocks * MOE_BLOCK
    row_tok = jnp.full((rows,), t, jnp.int32).at[dest].set(tok_sorted)
    row_gate = jnp.zeros((rows,), jnp.float32).at[dest].set(gate_sorted)
    block_start = jnp.arange(n_blocks) * MOE_BLOCK
    block_e = jnp.minimum(jnp.sum(block_start[:, None] >= (pstarts + padded)[None, :], axis=1), N_EXPERTS - 1)
    h_pad = jnp.concatenate([h, jnp.zeros((1, d), h.dtype)], axis=0)
    xb = h_pad[row_tok].reshape(n_blocks, MOE_BLOCK, d)

    def expert_block(args):
        xblk, e = args
        return (jax.nn.silu(xblk @ w_gate[e]) * (xblk @ w_up[e])) @ w_down[e]

    yb = lax.map(expert_block, (xb, block_e))
    y_rows = yb.reshape(rows, d) * row_gate[:, None].astype(yb.dtype)
    return jax.ops.segment_sum(y_rows, row_tok, num_segments=t + 1)[:t]


def hybrid_layer(x, positions, norm_attn_w, w_in, q_norm_w, k_norm_w, conv_w, conv_b,
                 dt_bias, a_log, d_skip, ssd_norm_w, w_out, norm_ffn_w,
                 router_group_w, router_group_b, router_expert_w, router_expert_b,
                 w_gate, w_up, w_down):
    b, s, d = x.shape
    h = rms_norm(x, norm_attn_w, NORM_EPS)
    proj = jnp.einsum('bsd,de->bse', h, w_in)
    q, k, v, z, xbc, dt_raw = jnp.split(proj, IN_SPLITS, axis=-1)
    q = rotary(rms_norm(q.reshape(b, s, ATTN_HEADS, ATTN_HEAD_DIM), q_norm_w, NORM_EPS), positions)
    k = rotary(rms_norm(k.reshape(b, s, ATTN_HEADS, ATTN_HEAD_DIM), k_norm_w, NORM_EPS), positions)
    v = v.reshape(b, s, ATTN_HEADS, ATTN_HEAD_DIM)
    attn = dilated_attention(q, k, v).reshape(b, s, ATTN_WIDTH)
    ssd = ssd_mixer(z, xbc, dt_raw, conv_w, conv_b, dt_bias, a_log, d_skip, ssd_norm_w)
    mix = jnp.concatenate([attn, ssd], axis=-1)
    x = x + jnp.einsum('bse,ed->bsd', mix, w_out)
    h2 = rms_norm(x, norm_ffn_w, NORM_EPS).reshape(b * s, d)
    y = hierarchical_moe(h2, router_group_w, router_group_b, router_expert_w, router_expert_b,
                         w_gate, w_up, w_down)
    return x + y.reshape(b, s, d)


def setup_inputs(seed: int = 0) -> dict:
    key = jax.random.key(seed)
    ks = jax.random.split(key, 24)
    f32 = jnp.float32
    nrm = lambda k, shape, sc: jax.random.normal(k, shape, f32) * sc
    x = jax.random.normal(ks[0], (BATCH, SEQ, D_MODEL), f32)
    positions = jnp.tile(jnp.arange(SEQ, dtype=jnp.int32)[None, :], (BATCH, 1))
    dt0 = jnp.exp(jax.random.uniform(ks[8], (DEPTH, SSD_HEADS), f32, np.log(1e-3), np.log(1e-1)))
    return {
        'x': x,
        'positions': positions,
        'norm_attn_w': 1.0 + nrm(ks[1], (DEPTH, D_MODEL), 0.02),
        'w_in': nrm(ks[2], (DEPTH, D_MODEL, IN_PROJ_WIDTH), D_MODEL ** -0.5),
        'q_norm_w': 1.0 + nrm(ks[3], (DEPTH, ATTN_HEAD_DIM), 0.02),
        'k_norm_w': 1.0 + nrm(ks[4], (DEPTH, ATTN_HEAD_DIM), 0.02),
        'conv_w': nrm(ks[5], (DEPTH, SSD_CONV, SSD_CONV_DIM), SSD_CONV ** -0.5),
        'conv_b': nrm(ks[6], (DEPTH, SSD_CONV_DIM), 0.01),
        'dt_bias': dt0 + jnp.log(-jnp.expm1(-dt0)),
        'a_log': jnp.log(jax.random.uniform(ks[9], (DEPTH, SSD_HEADS), f32, 1.0, 16.0)),
        'd_skip': 1.0 + nrm(ks[10], (DEPTH, SSD_HEADS), 0.1),
        'ssd_norm_w': 1.0 + nrm(ks[11], (DEPTH, SSD_WIDTH), 0.02),
        'w_out': nrm(ks[12], (DEPTH, MIX_WIDTH, D_MODEL), MIX_WIDTH ** -0.5),
        'norm_ffn_w': 1.0 + nrm(ks[13], (DEPTH, D_MODEL), 0.02),
        'router_group_w': nrm(ks[14], (DEPTH, D_MODEL, N_EXPERT_GROUPS), D_MODEL ** -0.5),
        'router_group_b': nrm(ks[15], (DEPTH, N_EXPERT_GROUPS), 0.01),
        'router_expert_w': nrm(ks[16], (DEPTH, D_MODEL, N_EXPERTS), D_MODEL ** -0.5),
        'router_expert_b': nrm(ks[17], (DEPTH, N_EXPERTS), 0.01),
        'w_gate': nrm(ks[18], (DEPTH, N_EXPERTS, D_MODEL, EXPERT_FF), D_MODEL ** -0.5),
        'w_up': nrm(ks[19], (DEPTH, N_EXPERTS, D_MODEL, EXPERT_FF), D_MODEL ** -0.5),
        'w_down': nrm(ks[20], (DEPTH, N_EXPERTS, EXPERT_FF, D_MODEL), EXPERT_FF ** -0.5),
    }


def reference(x, positions, norm_attn_w, w_in, q_norm_w, k_norm_w, conv_w, conv_b,
              dt_bias, a_log, d_skip, ssd_norm_w, w_out, norm_ffn_w,
              router_group_w, router_group_b, router_expert_w, router_expert_b,
              w_gate, w_up, w_down):
    for layer in range(DEPTH):
        x = hybrid_layer(x, positions, norm_attn_w[layer], w_in[layer], q_norm_w[layer],
                         k_norm_w[layer], conv_w[layer], conv_b[layer], dt_bias[layer],
                         a_log[layer], d_skip[layer], ssd_norm_w[layer], w_out[layer],
                         norm_ffn_w[layer], router_group_w[layer], router_group_b[layer],
                         router_expert_w[layer], router_expert_b[layer],
                         w_gate[layer], w_up[layer], w_down[layer])
    return x
```

```python
import functools

import jax
import jax.numpy as jnp
from jax import lax
from jax.experimental import pallas as pl
from jax.experimental.pallas import tpu as pltpu

F32 = jnp.float32
BF16 = jnp.bfloat16
I32 = jnp.int32
HIGHEST = lax.Precision.HIGHEST

HEAD_DIM = 128
DILATED_BRANCHES = ((128, 1), (512, 4), (2048, 16))
ATTN_BLOCK = 128
ROPE_THETA = 10000.0
SSD_HEAD_DIM = 64
SSD_GROUPS = 8
SSD_STATE = 128
SSD_CONV = 4
SSD_CHUNK = 256
N_EXPERT_GROUPS = 4
EXPERTS_PER_GROUP = 8
N_EXPERTS = N_EXPERT_GROUPS * EXPERTS_PER_GROUP
TOP_K = 2
NORM_EPS = 1e-6
SSD_NORM_EPS = 1e-5

LANES = 128
SUBLANES = 8
VMEM_LIMIT = 56 * 1024 * 1024

ROW_TILE = 512
MM_TM, MM_TN, MM_TK = 2048, 1024, 1024
MM_SUB = 256
OUT_TM = 1024
MOE_ROWS = 512
MOE_FF_TILE = 256
MOE_DOWN_SPLIT = 2
ROUTER_TILE = 256
ROUTER_EXPERT_LANE0 = N_EXPERT_GROUPS


def _silu(v):
    return v * (1.0 / (1.0 + jnp.exp(-v)))


def _nt_dot(a, b, **kw):
    return lax.dot_general(a, b, (((1,), (1,)), ((), ())), preferred_element_type=F32, **kw)


def _rmsnorm_cast_kernel(x_ref, w_ref, o_ref):
    x = x_ref[...]
    y = x * lax.rsqrt(jnp.mean(x * x, axis=-1, keepdims=True) + NORM_EPS)
    o_ref[...] = (y * w_ref[...]).astype(o_ref.dtype)


def _rmsnorm_cast(x2d, w):
    t, d = x2d.shape
    return pl.pallas_call(
        _rmsnorm_cast_kernel,
        grid=(t // ROW_TILE,),
        in_specs=[pl.BlockSpec((ROW_TILE, d), lambda i: (i, 0)),
                  pl.BlockSpec((1, d), lambda i: (0, 0))],
        out_specs=pl.BlockSpec((ROW_TILE, d), lambda i: (i, 0)),
        out_shape=jax.ShapeDtypeStruct((t, d), BF16),
        compiler_params=pltpu.CompilerParams(dimension_semantics=("parallel",), vmem_limit_bytes=VMEM_LIMIT),
        name="rmsnorm_cast",
    )(x2d, w.reshape(1, d))


def _accumulate_rows(a_ref, wb_ref, acc_ref):
    def body(s, c):
        rows = pl.ds(pl.multiple_of(s * MM_SUB, MM_SUB), MM_SUB)
        acc_ref[rows, :] += jnp.dot(a_ref[rows, :], wb_ref[...], preferred_element_type=F32)
        return c
    lax.fori_loop(0, a_ref.shape[0] // MM_SUB, body, 0)


def _matmul_kernel(a_ref, w_ref, o_ref, acc_ref, wb_ref):
    k = pl.program_id(2)
    wb_ref[...] = w_ref[...].astype(BF16)

    @pl.when(k == 0)
    def _():
        acc_ref[...] = jnp.zeros_like(acc_ref)

    _accumulate_rows(a_ref, wb_ref, acc_ref)

    @pl.when(k == pl.num_programs(2) - 1)
    def _():
        o_ref[...] = acc_ref[...].astype(o_ref.dtype)


def _matmul(a, w, n_cols, tm, tn, tk, name):
    m, kdim = a.shape
    return pl.pallas_call(
        _matmul_kernel,
        grid=(m // tm, n_cols // tn, kdim // tk),
        in_specs=[pl.BlockSpec((tm, tk), lambda i, j, k: (i, k)),
                  pl.BlockSpec((tk, tn), lambda i, j, k: (k, j))],
        out_specs=pl.BlockSpec((tm, tn), lambda i, j, k: (i, j)),
        out_shape=jax.ShapeDtypeStruct((m, n_cols), F32),
        scratch_shapes=[pltpu.VMEM((tm, tn), F32), pltpu.VMEM((tk, tn), BF16)],
        compiler_params=pltpu.CompilerParams(
            dimension_semantics=("parallel", "parallel", "arbitrary"), vmem_limit_bytes=VMEM_LIMIT),
        name=name,
    )(a, w)


def _out_proj_kernel(a1_ref, a2_ref, w_ref, x_ref, o_ref, acc_ref, wb_ref, *, k_half):
    k = pl.program_id(2)
    wb_ref[...] = w_ref[...].astype(BF16)

    @pl.when(k == 0)
    def _():
        acc_ref[...] = jnp.zeros_like(acc_ref)

    @pl.when(k < k_half)
    def _():
        _accumulate_rows(a1_ref, wb_ref, acc_ref)

    @pl.when(k >= k_half)
    def _():
        _accumulate_rows(a2_ref, wb_ref, acc_ref)

    @pl.when(k == pl.num_programs(2) - 1)
    def _():
        o_ref[...] = x_ref[...] + acc_ref[...]


def _out_proj(a1, a2, w, x2d, tm, tn, tk):
    m, khalf_dim = a1.shape
    kdim, n = w.shape
    k_half = khalf_dim // tk
    return pl.pallas_call(
        functools.partial(_out_proj_kernel, k_half=k_half),
        grid=(m // tm, n // tn, kdim // tk),
        in_specs=[pl.BlockSpec((tm, tk), lambda i, j, k: (i, jnp.minimum(k, k_half - 1))),
                  pl.BlockSpec((tm, tk), lambda i, j, k: (i, jnp.maximum(k - k_half, 0))),
                  pl.BlockSpec((tk, tn), lambda i, j, k: (k, j)),
                  pl.BlockSpec((tm, tn), lambda i, j, k: (i, j))],
        out_specs=pl.BlockSpec((tm, tn), lambda i, j, k: (i, j)),
        out_shape=jax.ShapeDtypeStruct((m, n), F32),
        scratch_shapes=[pltpu.VMEM((tm, tn), F32), pltpu.VMEM((tk, tn), BF16)],
        compiler_params=pltpu.CompilerParams(
            dimension_semantics=("parallel", "parallel", "arbitrary"), vmem_limit_bytes=VMEM_LIMIT),
        name="out_proj",
    )(a1, a2, w, x2d)


def _attn_kernel(pos_ref, rope_ref, qw_ref, kw_ref, q_ref, k_ref, v_ref, o_ref,
                 cos_ref, sin_ref, qf_ref, kf_ref, qd_ref, kd_ref, vd_ref,
                 acc_ref, m_ref, l_ref, accd_ref, md_ref, ld_ref):
    seq = q_ref.shape[0]
    blk = ATTN_BLOCK
    n_blocks = seq // blk

    @pl.when(pl.program_id(1) == 0)
    def _():
        ang = pos_ref[...].astype(F32) * rope_ref[0:1, :]
        cos_ref[...] = jnp.cos(ang)
        sin_ref[...] = jnp.sin(ang) * rope_ref[1:2, :]

    def norm_rot(t_ref, w_ref, dst_ref, scale):
        def body(i, c):
            rows = pl.ds(pl.multiple_of(i * 256, 256), 256)
            t = t_ref[rows, :]
            y = t * lax.rsqrt(jnp.mean(t * t, axis=-1, keepdims=True) + NORM_EPS) * w_ref[...]
            y = y * cos_ref[rows, :] + pltpu.roll(y, HEAD_DIM // 2, 1) * sin_ref[rows, :]
            dst_ref[rows, :] = y * scale if scale != 1.0 else y
            return c
        lax.fori_loop(0, seq // 256, body, 0)

    norm_rot(q_ref, qw_ref, qf_ref, HEAD_DIM ** -0.5)
    norm_rot(k_ref, kw_ref, kf_ref, 1.0)

    kd_ref[0:blk, :] = jnp.zeros((blk, HEAD_DIM), BF16)
    vd_ref[0:blk, :] = jnp.zeros((blk, HEAD_DIM), BF16)

    qi2 = lax.broadcasted_iota(I32, (blk, 2 * blk), 0)
    kj2 = lax.broadcasted_iota(I32, (blk, 2 * blk), 1)
    qi1 = lax.broadcasted_iota(I32, (blk, blk), 0)
    kj1 = lax.broadcasted_iota(I32, (blk, blk), 1)
    neg_inf = jnp.float32(-jnp.inf)

    for bi, (window, dil) in enumerate(DILATED_BRANCHES):
        n_back = window // dil
        sub_len = seq // dil
        nb = sub_len // blk
        natural = dil == 1
        for r in range(dil):
            src = slice(None) if natural else pl.ds(r, sub_len, stride=dil)
            dst = slice(r * sub_len, (r + 1) * sub_len)
            dstp = slice(blk + r * sub_len, blk + (r + 1) * sub_len)
            qd_ref[dst, :] = qf_ref[src, :].astype(BF16)
            kd_ref[dstp, :] = kf_ref[src, :].astype(BF16)
            vd_ref[dstp, :] = v_ref[src, :].astype(BF16)

        acc_out = acc_ref.at[bi] if natural else accd_ref
        m_out = m_ref.at[bi] if natural else md_ref
        l_out = l_ref.at[bi] if natural else ld_ref

        if nb == 1:
            dist1 = qi1 - kj1
            mask1 = (dist1 >= 0) & (dist1 <= n_back)

        dist2 = blk + qi2 - kj2
        mask2 = (dist2 >= 0) & (dist2 <= n_back)

        def block(j, c):
            r0 = pl.multiple_of(j * blk, blk)
            q = qd_ref[pl.ds(r0, blk), :]
            if nb == 1:
                kk = kd_ref[pl.ds(r0 + blk, blk), :]
                vv = vd_ref[pl.ds(r0 + blk, blk), :]
                s = jnp.where(mask1, _nt_dot(q, kk), neg_inf)
            else:
                kk = kd_ref[pl.ds(r0, 2 * blk), :]
                vv = vd_ref[pl.ds(r0, 2 * blk), :]
                first_lim = jnp.where((j & (nb - 1)) == 0, blk, 0)
                s = jnp.where(mask2 & (kj2 >= first_lim), _nt_dot(q, kk), neg_inf)
            m = jnp.max(s, axis=1, keepdims=True)
            p = jnp.exp(s - m)
            l = jnp.sum(p, axis=1, keepdims=True)
            acc = jnp.dot(p.astype(BF16), vv, preferred_element_type=F32)
            acc_out[pl.ds(r0, blk), :] = acc
            m_out[pl.ds(r0, blk), :] = jnp.broadcast_to(m, (blk, HEAD_DIM))
            l_out[pl.ds(r0, blk), :] = jnp.broadcast_to(l, (blk, HEAD_DIM))
            return c
        lax.fori_loop(0, n_blocks, block, 0, unroll=4)

        if not natural:
            for r in range(dil):
                src = slice(r * sub_len, (r + 1) * sub_len)
                dst = pl.ds(r, sub_len, stride=dil)
                acc_ref[bi, dst, :] = accd_ref[src, :]
                m_ref[bi, dst, :] = md_ref[src, :]
                l_ref[bi, dst, :] = ld_ref[src, :]

    def merge(i, c):
        rows = pl.ds(pl.multiple_of(i * 256, 256), 256)
        ms = [m_ref[b, rows, :] for b in range(len(DILATED_BRANCHES))]
        m_all = functools.reduce(jnp.maximum, ms)
        num = jnp.zeros((256, HEAD_DIM), F32)
        den = jnp.zeros((256, HEAD_DIM), F32)
        for b in range(len(DILATED_BRANCHES)):
            e = jnp.exp(ms[b] - m_all)
            num = num + e * acc_ref[b, rows, :]
            den = den + e * l_ref[b, rows, :]
        o_ref[rows, :] = (num / den).astype(o_ref.dtype)
        return c
    lax.fori_loop(0, seq // 256, merge, 0)


def _attention(proj, pos_col, rope_tab, q_norm_w, k_norm_w, batch, seq, n_heads):
    t = batch * seq
    nbr = len(DILATED_BRANCHES)
    qkv_spec = lambda off: pl.BlockSpec((seq, HEAD_DIM), lambda b, h: (b, off + h))
    return pl.pallas_call(
        _attn_kernel,
        grid=(batch, n_heads),
        in_specs=[pl.BlockSpec((seq, 1), lambda b, h: (b, 0)),
                  pl.BlockSpec((2, HEAD_DIM), lambda b, h: (0, 0)),
                  pl.BlockSpec((1, HEAD_DIM), lambda b, h: (0, 0)),
                  pl.BlockSpec((1, HEAD_DIM), lambda b, h: (0, 0)),
                  qkv_spec(0), qkv_spec(n_heads), qkv_spec(2 * n_heads)],
        out_specs=pl.BlockSpec((seq, HEAD_DIM), lambda b, h: (b, h)),
        out_shape=jax.ShapeDtypeStruct((t, n_heads * HEAD_DIM), BF16),
        scratch_shapes=[
            pltpu.VMEM((seq, HEAD_DIM), F32),
            pltpu.VMEM((seq, HEAD_DIM), F32),
            pltpu.VMEM((seq, HEAD_DIM), F32),
            pltpu.VMEM((seq, HEAD_DIM), F32),
            pltpu.VMEM((seq, HEAD_DIM), BF16),
            pltpu.VMEM((ATTN_BLOCK + seq, HEAD_DIM), BF16),
            pltpu.VMEM((ATTN_BLOCK + seq, HEAD_DIM), BF16),
            pltpu.VMEM((nbr, seq, HEAD_DIM), F32),
            pltpu.VMEM((nbr, seq, HEAD_DIM), F32),
            pltpu.VMEM((nbr, seq, HEAD_DIM), F32),
            pltpu.VMEM((seq, HEAD_DIM), F32),
            pltpu.VMEM((seq, HEAD_DIM), F32),
            pltpu.VMEM((seq, HEAD_DIM), F32),
        ],
        compiler_params=pltpu.CompilerParams(
            dimension_semantics=("parallel", "arbitrary"), vmem_limit_bytes=VMEM_LIMIT),
        name="dilated_attention",
    )(pos_col, rope_tab, q_norm_w.reshape(1, HEAD_DIM), k_norm_w.reshape(1, HEAD_DIM), proj, proj, proj)


def _ssd_kernel(z_ref, xbc_ref, dt_ref, convw_ref, convb_ref, dtb_ref, alog_ref, dskip_ref, normw_ref,
                o_ref, ext_ref, act_ref, state_ref, cs_ref, ecs_ref, cst_ref, wt_ref, dtt_ref, cdec_ref):
    lc = SSD_CHUNK
    width = o_ref.shape[1]
    gw = width // SSD_GROUPS
    heads_per_group = gw // SSD_HEAD_DIM
    conv_dim = xbc_ref.shape[1]
    halo = SUBLANES

    @pl.when(pl.program_id(1) == 0)
    def _():
        ext_ref[0:halo, :] = jnp.zeros((halo, conv_dim), F32)
        state_ref[...] = jnp.zeros_like(state_ref)

    ext_ref[halo:halo + lc, :] = xbc_ref[...]
    col_tile = 512
    for ct in range(conv_dim // col_tile):
        cols = slice(ct * col_tile, (ct + 1) * col_tile)
        acc = jnp.broadcast_to(convb_ref[:, cols], (lc, col_tile))
        for j in range(SSD_CONV):
            r0 = halo - (SSD_CONV - 1) + j
            acc = acc + convw_ref[j:j + 1, cols] * ext_ref[r0:r0 + lc, cols]
        act_ref[:, cols] = _silu(acc)
    ext_ref[0:halo, :] = xbc_ref[lc - halo:lc, :]

    x0 = dt_ref[...] + dtb_ref[...]
    dt = jnp.maximum(x0, 0.0) + jnp.log1p(jnp.exp(-jnp.abs(x0)))
    d_a = dt * (-jnp.exp(alog_ref[...]))
    tri = (lax.broadcasted_iota(I32, (lc, lc), 0) >= lax.broadcasted_iota(I32, (lc, lc), 1)).astype(F32)
    cs = jnp.dot(tri, d_a, precision=HIGHEST, preferred_element_type=F32)
    cs_last = cs[lc - 1:lc, :]
    cs_ref[...] = cs
    ecs_ref[...] = jnp.exp(cs)
    cdec_ref[...] = jnp.exp(cs_last)
    eye = (lax.broadcasted_iota(I32, (LANES, LANES), 0) == lax.broadcasted_iota(I32, (LANES, LANES), 1)).astype(F32)
    cst_ref[...] = _nt_dot(eye, cs, precision=HIGHEST)
    wt_ref[...] = _nt_dot(eye, jnp.exp(cs_last - cs) * dt, precision=HIGHEST)
    dtt_ref[...] = _nt_dot(eye, dt, precision=HIGHEST)

    causal = lax.broadcasted_iota(I32, (lc, lc), 0) >= lax.broadcasted_iota(I32, (lc, lc), 1)
    lane_head = lax.broadcasted_iota(I32, (lc, gw), 1) // SSD_HEAD_DIM
    lane_head_s = lax.broadcasted_iota(I32, (SSD_STATE, gw), 1) // SSD_HEAD_DIM
    n_xs = width
    n_b = SSD_GROUPS * SSD_STATE

    for g in range(SSD_GROUPS):
        gc = slice(g * gw, (g + 1) * gw)
        xs = act_ref[:, gc]
        b_g = act_ref[:, n_xs + g * SSD_STATE:n_xs + (g + 1) * SSD_STATE]
        c_g = act_ref[:, n_xs + n_b + g * SSD_STATE:n_xs + n_b + (g + 1) * SSD_STATE]
        xs_b = xs.astype(BF16)
        c_b = c_g.astype(BF16)
        cb = _nt_dot(c_b, b_g.astype(BF16))
        b_t = jnp.transpose(b_g)
        s_prev = state_ref[g]
        y_off = jnp.dot(c_b, s_prev.astype(BF16), preferred_element_type=F32)
        y = jnp.zeros((lc, gw), F32)
        s_new = jnp.zeros((SSD_STATE, gw), F32)
        for r in range(heads_per_group):
            h = g * heads_per_group + r
            diff = cs_ref[:, h:h + 1] - cst_ref[h:h + 1, :]
            seg = jnp.exp(jnp.where(causal, diff, -jnp.inf))
            m_h = (cb * seg * dtt_ref[h:h + 1, :]).astype(BF16)
            y_h = jnp.dot(m_h, xs_b, preferred_element_type=F32)
            bw = (b_t * wt_ref[h:h + 1, :]).astype(BF16)
            s_h = jnp.dot(bw, xs_b, preferred_element_type=F32)
            y = jnp.where(lane_head == r, y_h + ecs_ref[:, h:h + 1] * y_off, y)
            s_new = jnp.where(lane_head_s == r, cdec_ref[:, h:h + 1] * s_prev + s_h, s_new)
        state_ref[g] = s_new
        y = y + dskip_ref[:, gc] * xs
        y = y * _silu(z_ref[:, gc])
        y = y * lax.rsqrt(jnp.mean(y * y, axis=-1, keepdims=True) + SSD_NORM_EPS)
        o_ref[:, gc] = (y * normw_ref[:, gc]).astype(o_ref.dtype)


def _ssd(proj, dt_raw, conv_w, conv_b, dt_bias_pad, a_log_pad, d_skip_ch, norm_w, batch, seq, z_col, xbc_col, width):
    t = batch * seq
    nc = seq // SSD_CHUNK
    conv_dim = conv_w.shape[1]
    gw = width // SSD_GROUPS
    row = lambda b, c: b * nc + c
    return pl.pallas_call(
        _ssd_kernel,
        grid=(batch, nc),
        in_specs=[pl.BlockSpec((SSD_CHUNK, width), lambda b, c: (row(b, c), z_col)),
                  pl.BlockSpec((SSD_CHUNK, conv_dim), lambda b, c: (row(b, c), xbc_col)),
                  pl.BlockSpec((SSD_CHUNK, LANES), lambda b, c: (row(b, c), 0)),
                  pl.BlockSpec((SSD_CONV, conv_dim), lambda b, c: (0, 0)),
                  pl.BlockSpec((1, conv_dim), lambda b, c: (0, 0)),
                  pl.BlockSpec((1, LANES), lambda b, c: (0, 0)),
                  pl.BlockSpec((1, LANES), lambda b, c: (0, 0)),
                  pl.BlockSpec((1, width), lambda b, c: (0, 0)),
                  pl.BlockSpec((1, width), lambda b, c: (0, 0))],
        out_specs=pl.BlockSpec((SSD_CHUNK, width), lambda b, c: (row(b, c), 0)),
        out_shape=jax.ShapeDtypeStruct((t, width), BF16),
        scratch_shapes=[
            pltpu.VMEM((SUBLANES + SSD_CHUNK, conv_dim), F32),
            pltpu.VMEM((SSD_CHUNK, conv_dim), F32),
            pltpu.VMEM((SSD_GROUPS, SSD_STATE, gw), F32),
            pltpu.VMEM((SSD_CHUNK, LANES), F32),
            pltpu.VMEM((SSD_CHUNK, LANES), F32),
            pltpu.VMEM((LANES, SSD_CHUNK), F32),
            pltpu.VMEM((LANES, SSD_CHUNK), F32),
            pltpu.VMEM((LANES, SSD_CHUNK), F32),
            pltpu.VMEM((1, LANES), F32),
        ],
        compiler_params=pltpu.CompilerParams(
            dimension_semantics=("parallel", "arbitrary"), vmem_limit_bytes=VMEM_LIMIT),
        name="ssd_mixer",
    )(proj, proj, dt_raw, conv_w, conv_b.reshape(1, conv_dim), dt_bias_pad, a_log_pad,
      d_skip_ch.reshape(1, width), norm_w.reshape(1, width))


def _router_kernel(x_ref, nw_ref, wr_ref, br_ref, h_ref, meta_ref, cnt_ref, carry_ref):
    tm = x_ref.shape[0]
    n_slabs = h_ref.shape[1]

    @pl.when(pl.program_id(0) == 0)
    def _():
        carry_ref[...] = jnp.zeros_like(carry_ref)

    x = x_ref[...]
    h = x * lax.rsqrt(jnp.mean(x * x, axis=-1, keepdims=True) + NORM_EPS) * nw_ref[...]
    for s in range(n_slabs):
        h_ref[:, s, :] = h[:, s * LANES:(s + 1) * LANES]

    logits = jnp.dot(h, wr_ref[...], precision=HIGHEST, preferred_element_type=F32) + br_ref[...]
    lane = lax.broadcasted_iota(I32, (tm, LANES), 1)
    neg_inf = jnp.float32(-jnp.inf)

    lane_f = lane.astype(F32)

    def first_argmax(v, vmax):
        return jnp.min(jnp.where(v == vmax, lane_f, float(LANES)), axis=1, keepdims=True).astype(I32)

    lg = jnp.where(lane < N_EXPERT_GROUPS, logits, neg_inf)
    g_max = jnp.max(lg, axis=1, keepdims=True)
    g_idx = first_argmax(lg, g_max)
    g_p = 1.0 / jnp.sum(jnp.exp(lg - g_max), axis=1, keepdims=True)

    e_lo = ROUTER_EXPERT_LANE0 + EXPERTS_PER_GROUP * g_idx
    le = jnp.where((lane >= e_lo) & (lane < e_lo + EXPERTS_PER_GROUP), logits, neg_inf)
    e_max = jnp.max(le, axis=1, keepdims=True)
    i1 = first_argmax(le, e_max)
    e_sum = jnp.sum(jnp.exp(le - e_max), axis=1, keepdims=True)
    le2 = jnp.where(lane == i1, neg_inf, le)
    e_max2 = jnp.max(le2, axis=1, keepdims=True)
    i2 = first_argmax(le2, e_max2)
    p1 = 1.0 / e_sum
    p2 = jnp.exp(e_max2 - e_max) / e_sum
    gate1 = g_p * (p1 / (p1 + p2))
    gate2 = g_p * (p2 / (p1 + p2))

    hot = ((lane == i1) | (lane == i2))
    hot_b = hot.astype(BF16)
    strict = (lax.broadcasted_iota(I32, (tm, tm), 0) > lax.broadcasted_iota(I32, (tm, tm), 1)).astype(BF16)
    before = jnp.dot(strict, hot_b, preferred_element_type=F32) + carry_ref[0:1, :]
    rank1 = jnp.sum(jnp.where(lane == i1, before, 0.0), axis=1, keepdims=True)
    rank2 = jnp.sum(jnp.where(lane == i2, before, 0.0), axis=1, keepdims=True)
    carry_ref[0:1, :] = carry_ref[0:1, :] + jnp.sum(hot.astype(F32), axis=0, keepdims=True)
    cnt_ref[...] = jnp.broadcast_to(carry_ref[0:1, :], cnt_ref.shape)

    vals = [(i1 - ROUTER_EXPERT_LANE0).astype(F32), (i2 - ROUTER_EXPERT_LANE0).astype(F32), rank1, rank2, gate1, gate2]
    meta = jnp.zeros((tm, LANES), F32)
    for c, v in enumerate(vals):
        meta = jnp.where(lane == c, v, meta)
    meta_ref[...] = meta


def _router(x1, norm_w, w_router, b_router):
    t, d = x1.shape
    n_slabs = d // LANES
    return pl.pallas_call(
        _router_kernel,
        grid=(t // ROUTER_TILE,),
        in_specs=[pl.BlockSpec((ROUTER_TILE, d), lambda i: (i, 0)),
                  pl.BlockSpec((1, d), lambda i: (0, 0)),
                  pl.BlockSpec((d, LANES), lambda i: (0, 0)),
                  pl.BlockSpec((1, LANES), lambda i: (0, 0))],
        out_specs=[pl.BlockSpec((ROUTER_TILE, n_slabs, LANES), lambda i: (i, 0, 0)),
                   pl.BlockSpec((ROUTER_TILE, LANES), lambda i: (i, 0)),
                   pl.BlockSpec((SUBLANES, LANES), lambda i: (0, 0))],
        out_shape=[jax.ShapeDtypeStruct((t, n_slabs, LANES), F32),
                   jax.ShapeDtypeStruct((t, LANES), F32),
                   jax.ShapeDtypeStruct((SUBLANES, LANES), F32)],
        scratch_shapes=[pltpu.VMEM((SUBLANES, LANES), F32)],
        compiler_params=pltpu.CompilerParams(dimension_semantics=("arbitrary",), vmem_limit_bytes=VMEM_LIMIT),
        name="router",
    )(x1, norm_w.reshape(1, d), w_router, b_router)


def _row_copy(src_ref, dst_ref, src_row, dst_row, sem):
    return pltpu.make_async_copy(src_ref.at[src_row], dst_ref.at[dst_row], sem)


def _moe_up_kernel(n_items_ref, item_e_ref, item_start_ref, item_n_ref, flat_ref,
                   h_hbm, wg_ref, wu_ref, o_ref, xbuf_ref, xb_ref, wgb_ref, wub_ref, sem,
                   *, n_tokens):
    i = pl.program_id(0)
    j = pl.program_id(1)
    valid = i < n_items_ref[0]
    rows = xbuf_ref.shape[0]
    n_slabs = xbuf_ref.shape[1]
    sub = ATTN_BLOCK

    @pl.when((i == 0) & (j == 0))
    def _():
        xbuf_ref[...] = jnp.zeros_like(xbuf_ref)

    @pl.when(valid & (j == 0))
    def _():
        start = item_start_ref[i]
        n = item_n_ref[i]

        def issue(r, c):
            tok = flat_ref[start + r] & (n_tokens - 1)
            _row_copy(h_hbm, xbuf_ref, tok, r, sem).start()
            return c
        lax.fori_loop(0, n, issue, 0)

        def wait(r, c):
            _row_copy(h_hbm, xbuf_ref, 0, 0, sem).wait()
            return c
        lax.fori_loop(0, n, wait, 0)
        for s in range(n_slabs):
            xb_ref[:, s * LANES:(s + 1) * LANES] = xbuf_ref[:, s, :].astype(BF16)

    o_ref[...] = jnp.zeros_like(o_ref)

    @pl.when(valid)
    def _():
        n = item_n_ref[i]
        wgb_ref[...] = wg_ref[...].astype(BF16)
        wub_ref[...] = wu_ref[...].astype(BF16)

        def body(s, c):
            r0 = pl.multiple_of(s * sub, sub)
            xs = xb_ref[pl.ds(r0, sub), :]
            g = jnp.dot(xs, wgb_ref[...], preferred_element_type=F32)
            u = jnp.dot(xs, wub_ref[...], preferred_element_type=F32)
            o_ref[pl.ds(r0, sub), :] = (_silu(g) * u).astype(o_ref.dtype)
            return c
        lax.fori_loop(0, (n + sub - 1) // sub, body, 0)


def _moe_up(sched, h_slabs, w_gate, w_up, n_items_max):
    n_tokens, n_slabs, _ = h_slabs.shape
    _, d, ff = w_gate.shape
    nj = ff // MOE_FF_TILE

    def w_map(i, j, n_items, item_e, *_):
        ii = jnp.minimum(i, n_items[0] - 1)
        return (item_e[ii], 0, jnp.where(i < n_items[0], j, nj - 1))

    def o_map(i, j, *_):
        return (i, j)

    return pl.pallas_call(
        functools.partial(_moe_up_kernel, n_tokens=n_tokens),
        grid_spec=pltpu.PrefetchScalarGridSpec(
            num_scalar_prefetch=5,
            grid=(n_items_max, nj),
            in_specs=[pl.BlockSpec(memory_space=pl.ANY),
                      pl.BlockSpec((None, d, MOE_FF_TILE), w_map),
                      pl.BlockSpec((None, d, MOE_FF_TILE), w_map)],
            out_specs=pl.BlockSpec((MOE_ROWS, MOE_FF_TILE), o_map),
            scratch_shapes=[pltpu.VMEM((MOE_ROWS, n_slabs, LANES), F32),
                            pltpu.VMEM((MOE_ROWS, d), BF16),
                            pltpu.VMEM((d, MOE_FF_TILE), BF16),
                            pltpu.VMEM((d, MOE_FF_TILE), BF16),
                            pltpu.SemaphoreType.DMA(())]),
        out_shape=jax.ShapeDtypeStruct((n_items_max * MOE_ROWS, ff), BF16),
        compiler_params=pltpu.CompilerParams(
            dimension_semantics=("arbitrary", "arbitrary"), vmem_limit_bytes=VMEM_LIMIT),
        name="moe_up",
    )(*sched, h_slabs, w_gate, w_up)


def _moe_down_kernel(n_items_ref, item_e_ref, item_start_ref, item_n_ref, flat_ref,
                     h_ref, wd_ref, y_hbm, ybuf_ref, wdb_ref, sem):
    i = pl.program_id(0)
    j = pl.program_id(1)
    valid = i < n_items_ref[0]
    tn = wd_ref.shape[1]
    slabs_per_step = tn // LANES
    sub = ATTN_BLOCK

    @pl.when(valid)
    def _():
        n = item_n_ref[i]
        wdb_ref[...] = wd_ref[...].astype(BF16)
        for jj in range(MOE_DOWN_SPLIT):
            @pl.when(j == jj)
            def _():
                def body(s, c):
                    r0 = pl.multiple_of(s * sub, sub)
                    y = jnp.dot(h_ref[pl.ds(r0, sub), :], wdb_ref[...], preferred_element_type=F32)
                    for q in range(slabs_per_step):
                        ybuf_ref[pl.ds(r0, sub), jj * slabs_per_step + q, :] = y[:, q * LANES:(q + 1) * LANES]
                    return c
                lax.fori_loop(0, (n + sub - 1) // sub, body, 0)

    @pl.when(valid & (j == MOE_DOWN_SPLIT - 1))
    def _():
        start = item_start_ref[i]
        n = item_n_ref[i]

        def issue(r, c):
            _row_copy(ybuf_ref, y_hbm, r, flat_ref[start + r], sem).start()
            return c
        lax.fori_loop(0, n, issue, 0)

        def wait(r, c):
            _row_copy(ybuf_ref, y_hbm, 0, 0, sem).wait()
            return c
        lax.fori_loop(0, n, wait, 0)


def _moe_down(sched, h_items, w_down, n_items_max, n_tokens):
    _, ff, d = w_down.shape
    tn = d // MOE_DOWN_SPLIT
    n_slabs = d // LANES

    def w_map(i, j, n_items, item_e, *_):
        ii = jnp.minimum(i, n_items[0] - 1)
        return (item_e[ii], 0, jnp.where(i < n_items[0], j, MOE_DOWN_SPLIT - 1))

    def h_map(i, j, n_items, *_):
        return (jnp.minimum(i, n_items[0] - 1), 0)

    return pl.pallas_call(
        _moe_down_kernel,
        grid_spec=pltpu.PrefetchScalarGridSpec(
            num_scalar_prefetch=5,
            grid=(n_items_max, MOE_DOWN_SPLIT),
            in_specs=[pl.BlockSpec((MOE_ROWS, ff), h_map),
                      pl.BlockSpec((None, ff, tn), w_map)],
            out_specs=pl.BlockSpec(memory_space=pl.ANY),
            scratch_shapes=[pltpu.VMEM((MOE_ROWS, n_slabs, LANES), F32),
                            pltpu.VMEM((ff, tn), BF16),
                            pltpu.SemaphoreType.DMA(())]),
        out_shape=jax.ShapeDtypeStruct((TOP_K * n_tokens, n_slabs, LANES), F32),
        compiler_params=pltpu.CompilerParams(
            dimension_semantics=("arbitrary", "arbitrary"), vmem_limit_bytes=VMEM_LIMIT),
        name="moe_down",
    )(*sched, h_items, w_down)


def _combine_kernel(x_ref, meta_ref, y0_ref, y1_ref, o_ref):
    n_slabs = y0_ref.shape[1]
    g1 = meta_ref[:, 4:5]
    g2 = meta_ref[:, 5:6]
    for s in range(n_slabs):
        cols = slice(s * LANES, (s + 1) * LANES)
        o_ref[:, cols] = x_ref[:, cols] + g1 * y0_ref[:, s, :] + g2 * y1_ref[:, s, :]


def _combine(x1, meta, y_slots):
    t, d = x1.shape
    n_slabs = d // LANES
    tm = ROUTER_TILE
    nt = t // tm
    return pl.pallas_call(
        _combine_kernel,
        grid=(nt,),
        in_specs=[pl.BlockSpec((tm, d), lambda i: (i, 0)),
                  pl.BlockSpec((tm, LANES), lambda i: (i, 0)),
                  pl.BlockSpec((tm, n_slabs, LANES), lambda i: (i, 0, 0)),
                  pl.BlockSpec((tm, n_slabs, LANES), lambda i: (nt + i, 0, 0))],
        out_specs=pl.BlockSpec((tm, d), lambda i: (i, 0)),
        out_shape=jax.ShapeDtypeStruct((t, d), F32),
        compiler_params=pltpu.CompilerParams(dimension_semantics=("parallel",), vmem_limit_bytes=VMEM_LIMIT),
        name="moe_combine",
    )(x1, meta, y_slots, y_slots)


def _moe_schedule(meta, counts_f, n_tokens, n_items_max):
    e_id = meta[:, 0:TOP_K].astype(I32)
    rank = meta[:, TOP_K:2 * TOP_K].astype(I32)
    counts = counts_f[0, ROUTER_EXPERT_LANE0:ROUTER_EXPERT_LANE0 + N_EXPERTS].astype(I32)
    starts = jnp.cumsum(counts) - counts
    pos = starts[e_id] + rank
    flat = jnp.arange(TOP_K, dtype=I32)[None, :] * n_tokens + jnp.arange(n_tokens, dtype=I32)[:, None]
    flat_sorted = jnp.zeros((TOP_K * n_tokens,), I32).at[pos.reshape(-1)].set(flat.reshape(-1))
    chunks = (counts + MOE_ROWS - 1) // MOE_ROWS
    chunk_end = jnp.cumsum(chunks)
    n_items = chunk_end[-1]
    item = jnp.arange(n_items_max, dtype=I32)
    item_e = jnp.minimum(jnp.sum(item[:, None] >= chunk_end[None, :], axis=1), N_EXPERTS - 1).astype(I32)
    local = item - (chunk_end - chunks)[item_e]
    item_start = starts[item_e] + local * MOE_ROWS
    item_n = jnp.clip(counts[item_e] - local * MOE_ROWS, 0, MOE_ROWS)
    in_range = item < n_items
    item_start = jnp.where(in_range, item_start, 0).astype(I32)
    item_n = jnp.where(in_range, item_n, 0).astype(I32)
    return (n_items.reshape(1).astype(I32), item_e, item_start, item_n, flat_sorted)


def _layer(x, positions, norm_attn_w, w_in, q_norm_w, k_norm_w, conv_w, conv_b, dt_bias, a_log, d_skip,
           ssd_norm_w, w_out, norm_ffn_w, router_group_w, router_group_b, router_expert_w, router_expert_b,
           w_gate, w_up, w_down):
    batch, seq, d = x.shape
    t = batch * seq
    attn_width = d // 2
    n_heads = attn_width // HEAD_DIM
    ssd_width = d - attn_width
    ssd_heads = ssd_width // SSD_HEAD_DIM
    conv_dim = ssd_width + 2 * SSD_GROUPS * SSD_STATE
    main_cols = 3 * attn_width + ssd_width + conv_dim
    assert w_in.shape[1] == main_cols + ssd_heads and ssd_heads <= LANES
    assert seq % (SSD_CHUNK) == 0 and t % MM_TM == 0 and (t & (t - 1)) == 0

    x2d = x.reshape(t, d)
    h = _rmsnorm_cast(x2d, norm_attn_w)
    proj = _matmul(h, w_in, main_cols, MM_TM, MM_TN, MM_TK, "in_proj")
    w_dt = jnp.pad(w_in[:, main_cols:], ((0, 0), (0, LANES - ssd_heads)))
    dt_raw = _matmul(h, w_dt, LANES, MM_TM, LANES, MM_TK, "dt_proj")

    half = HEAD_DIM // 2
    inv_freq = jnp.power(jnp.float32(ROPE_THETA), -jnp.arange(half, dtype=F32) / half)
    rope_tab = jnp.stack([jnp.concatenate([inv_freq, inv_freq]),
                          jnp.concatenate([-jnp.ones((half,), F32), jnp.ones((half,), F32)])])
    attn = _attention(proj, positions.reshape(t, 1), rope_tab, q_norm_w, k_norm_w, batch, seq, n_heads)

    pad_heads = lambda v: jnp.pad(v.astype(F32), (0, LANES - ssd_heads)).reshape(1, LANES)
    ssd = _ssd(proj, dt_raw, conv_w, conv_b, pad_heads(dt_bias), pad_heads(a_log),
               jnp.repeat(d_skip.astype(F32), SSD_HEAD_DIM), ssd_norm_w, batch, seq,
               z_col=(3 * attn_width) // ssd_width, xbc_col=(3 * attn_width + ssd_width) // conv_dim,
               width=ssd_width)

    x1 = _out_proj(attn, ssd, w_out, x2d, OUT_TM, MM_TN, MM_TK)

    n_router = N_EXPERT_GROUPS + N_EXPERTS
    w_router = jnp.pad(jnp.concatenate([router_group_w, router_expert_w], axis=1), ((0, 0), (0, LANES - n_router)))
    b_router = jnp.pad(jnp.concatenate([router_group_b, router_expert_b]), (0, LANES - n_router)).reshape(1, LANES)
    h2_slabs, meta, counts_f = _router(x1, norm_ffn_w, w_router, b_router)

    n_items_max = N_EXPERTS + (TOP_K * t) // MOE_ROWS
    sched = _moe_schedule(meta, counts_f, t, n_items_max)
    h_items = _moe_up(sched, h2_slabs, w_gate, w_up, n_items_max)
    y_slots = _moe_down(sched, h_items, w_down, n_items_max, t)
    out = _combine(x1, meta, y_slots)
    return out.reshape(batch, seq, d)


def kernel(x, positions, norm_attn_w, w_in, q_norm_w, k_norm_w, conv_w, conv_b, dt_bias, a_log, d_skip, ssd_norm_w, w_out, norm_ffn_w, router_group_w, router_group_b, router_expert_w, router_expert_b, w_gate, w_up, w_down):
    for layer in range(norm_attn_w.shape[0]):
        x = _layer(x, positions, norm_attn_w[layer], w_in[layer], q_norm_w[layer], k_norm_w[layer],
                   conv_w[layer], conv_b[layer], dt_bias[layer], a_log[layer], d_skip[layer],
                   ssd_norm_w[layer], w_out[layer], norm_ffn_w[layer], router_group_w[layer],
                   router_group_b[layer], router_expert_w[layer], router_expert_b[layer],
                   w_gate[layer], w_up[layer], w_down[layer])
    return x
```

```python
import functools

import jax
import jax.numpy as jnp
from jax import lax
from jax.experimental import pallas as pl
from jax.experimental.pallas import tpu as pltpu

F32 = jnp.float32
BF16 = jnp.bfloat16
I32 = jnp.int32
HIGHEST = lax.Precision.HIGHEST

HEAD_DIM = 128
DILATED_BRANCHES = ((128, 1), (512, 4), (2048, 16))
ATTN_BLOCK = 128
ROPE_THETA = 10000.0
SSD_HEAD_DIM = 64
SSD_GROUPS = 8
SSD_STATE = 128
SSD_CONV = 4
SSD_CHUNK = 256
N_EXPERT_GROUPS = 4
EXPERTS_PER_GROUP = 8
N_EXPERTS = N_EXPERT_GROUPS * EXPERTS_PER_GROUP
TOP_K = 2
NORM_EPS = 1e-6
SSD_NORM_EPS = 1e-5

LANES = 128
SUBLANES = 8
VMEM_LIMIT = 56 * 1024 * 1024

ROW_TILE = 512
IN_TM, IN_TN = 1024, 1024
OUT_TM, OUT_TN = 1024, 512
MOE_ROWS = 1024
MOE_SUB = 256
ROW_UNROLL = 8
MOE_FF_TILE = 256
MOE_DOWN_SPLIT = 2
ROUTER_TILE = 256
ROUTER_EXPERT_LANE0 = N_EXPERT_GROUPS


def _silu(v):
    return v * (1.0 / (1.0 + jnp.exp(-v)))


def _nt_dot(a, b, **kw):
    return lax.dot_general(a, b, (((1,), (1,)), ((), ())), preferred_element_type=F32, **kw)


def _rmsnorm_cast_kernel(x_ref, w_ref, o_ref):
    x = x_ref[...]
    y = x * lax.rsqrt(jnp.mean(x * x, axis=-1, keepdims=True) + NORM_EPS)
    o_ref[...] = (y * w_ref[...]).astype(o_ref.dtype)


def _rmsnorm_cast(x2d, w):
    t, d = x2d.shape
    return pl.pallas_call(
        _rmsnorm_cast_kernel,
        grid=(t // ROW_TILE,),
        in_specs=[pl.BlockSpec((ROW_TILE, d), lambda i: (i, 0)),
                  pl.BlockSpec((1, d), lambda i: (0, 0))],
        out_specs=pl.BlockSpec((ROW_TILE, d), lambda i: (i, 0)),
        out_shape=jax.ShapeDtypeStruct((t, d), BF16),
        compiler_params=pltpu.CompilerParams(dimension_semantics=("parallel",), vmem_limit_bytes=VMEM_LIMIT),
        name="rmsnorm_cast",
    )(x2d, w.reshape(1, d))


def _matmul_nt_kernel(a_ref, wt_ref, o_ref):
    o_ref[...] = _nt_dot(a_ref[...], wt_ref[...]).astype(o_ref.dtype)


def _matmul_nt(a, wt, n_rows, tm, tn, name):
    m, kdim = a.shape
    return pl.pallas_call(
        _matmul_nt_kernel,
        grid=(m // tm, n_rows // tn),
        in_specs=[pl.BlockSpec((tm, kdim), lambda i, j: (i, 0)),
                  pl.BlockSpec((tn, kdim), lambda i, j: (j, 0))],
        out_specs=pl.BlockSpec((tm, tn), lambda i, j: (i, j)),
        out_shape=jax.ShapeDtypeStruct((m, n_rows), F32),
        compiler_params=pltpu.CompilerParams(
            dimension_semantics=("parallel", "parallel"), vmem_limit_bytes=VMEM_LIMIT),
        name=name,
    )(a, wt)


def _out_proj_kernel(a1_ref, a2_ref, w_ref, x_ref, o_ref):
    k1 = a1_ref.shape[1]
    acc = jnp.dot(a1_ref[...], w_ref[0:k1, :], preferred_element_type=F32)
    acc = acc + jnp.dot(a2_ref[...], w_ref[k1:, :], preferred_element_type=F32)
    o_ref[...] = x_ref[...] + acc


def _out_proj(a1, a2, w, x2d, tm, tn):
    m, k1 = a1.shape
    k2 = a2.shape[1]
    kdim, n = w.shape
    return pl.pallas_call(
        _out_proj_kernel,
        grid=(m // tm, n // tn),
        in_specs=[pl.BlockSpec((tm, k1), lambda i, j: (i, 0)),
                  pl.BlockSpec((tm, k2), lambda i, j: (i, 0)),
                  pl.BlockSpec((kdim, tn), lambda i, j: (0, j)),
                  pl.BlockSpec((tm, tn), lambda i, j: (i, j))],
        out_specs=pl.BlockSpec((tm, tn), lambda i, j: (i, j)),
        out_shape=jax.ShapeDtypeStruct((m, n), F32),
        compiler_params=pltpu.CompilerParams(
            dimension_semantics=("parallel", "parallel"), vmem_limit_bytes=VMEM_LIMIT),
        name="out_proj",
    )(a1, a2, w, x2d)


def _attn_kernel(pos_ref, rope_ref, qw_ref, kw_ref, q_ref, k_ref, v_ref, o_ref,
                 cos_ref, sin_ref, qf_ref, kf_ref, qd_ref, kd_ref, vd_ref,
                 bias_ref, s_ref, p_ref, m_ref, o_br_ref, lse_br_ref, od_ref, lsed_ref):
    seq = q_ref.shape[0]
    blk = ATTN_BLOCK
    n_blocks = seq // blk

    @pl.when(pl.program_id(1) == 0)
    def _():
        ang = pos_ref[...].astype(F32) * rope_ref[0:1, :]
        cos_ref[...] = jnp.cos(ang)
        sin_ref[...] = jnp.sin(ang) * rope_ref[1:2, :]

    def norm_rot(t_ref, w_ref, dst_ref, scale):
        def body(i, c):
            rows = pl.ds(pl.multiple_of(i * 256, 256), 256)
            t = t_ref[rows, :]
            y = t * lax.rsqrt(jnp.mean(t * t, axis=-1, keepdims=True) + NORM_EPS) * w_ref[...]
            y = y * cos_ref[rows, :] + pltpu.roll(y, HEAD_DIM // 2, 1) * sin_ref[rows, :]
            dst_ref[rows, :] = y * scale if scale != 1.0 else y
            return c
        lax.fori_loop(0, seq // 256, body, 0, unroll=2)

    norm_rot(q_ref, qw_ref, qf_ref, HEAD_DIM ** -0.5)
    norm_rot(k_ref, kw_ref, kf_ref, 1.0)

    kd_ref[0:blk, :] = jnp.zeros((blk, HEAD_DIM), BF16)
    vd_ref[0:blk, 0:HEAD_DIM] = jnp.zeros((blk, HEAD_DIM), BF16)
    vd_ref[:, HEAD_DIM:] = jnp.ones((blk + seq, HEAD_DIM), BF16)

    for bi, (window, dil) in enumerate(DILATED_BRANCHES):
        n_back = window // dil
        sub_len = seq // dil
        nb = sub_len // blk
        natural = dil == 1
        kw = blk if nb == 1 else 2 * blk
        k_off = blk if nb == 1 else 0
        for r in range(dil):
            src = slice(None) if natural else pl.ds(r, sub_len, stride=dil)
            dst = slice(r * sub_len, (r + 1) * sub_len)
            dstp = slice(blk + r * sub_len, blk + (r + 1) * sub_len)
            qd_ref[dst, :] = qf_ref[src, :].astype(BF16)
            kd_ref[dstp, :] = kf_ref[src, :].astype(BF16)
            vd_ref[dstp, 0:HEAD_DIM] = v_ref[src, :].astype(BF16)

        o_out = o_br_ref.at[bi] if natural else od_ref
        lse_out = lse_br_ref.at[bi] if natural else lsed_ref

        qi = lax.broadcasted_iota(I32, (blk, kw), 0)
        kj = lax.broadcasted_iota(I32, (blk, kw), 1)
        dist = (blk - k_off) + qi - kj
        ok = (dist >= 0) & (dist <= n_back)
        bias_ref[0, :, 0:kw] = jnp.where(ok, 0.0, -jnp.inf)
        bias_ref[1, :, 0:kw] = jnp.where(ok & (kj >= blk - k_off), 0.0, -jnp.inf)

        def scores(j, c):
            r0 = pl.multiple_of(j * blk, blk)
            s_ref[pl.ds(r0, blk), 0:kw] = _nt_dot(qd_ref[pl.ds(r0, blk), :], kd_ref[pl.ds(r0 + k_off, kw), :])
            return c
        lax.fori_loop(0, n_blocks, scores, 0, unroll=8)

        def softmax(j, c):
            r0 = pl.multiple_of(j * blk, blk)
            first = jnp.where((j & (nb - 1)) == 0, 1, 0)
            s = s_ref[pl.ds(r0, blk), 0:kw] + bias_ref[first, :, 0:kw]
            m = jnp.max(s, axis=1, keepdims=True)
            p_ref[pl.ds(r0, blk), 0:kw] = jnp.exp(s - m).astype(BF16)
            m_ref[pl.ds(r0, blk), :] = jnp.broadcast_to(m, (blk, HEAD_DIM))
            return c
        lax.fori_loop(0, n_blocks, softmax, 0, unroll=4)

        def weighted(j, c):
            r0 = pl.multiple_of(j * blk, blk)
            acc = jnp.dot(p_ref[pl.ds(r0, blk), 0:kw], vd_ref[pl.ds(r0 + k_off, kw), :], preferred_element_type=F32)
            l = acc[:, HEAD_DIM:]
            o_out[pl.ds(r0, blk), :] = acc[:, 0:HEAD_DIM] / l
            lse_out[pl.ds(r0, blk), :] = m_ref[pl.ds(r0, blk), :] + jnp.log(l)
            return c
        lax.fori_loop(0, n_blocks, weighted, 0, unroll=8)

        if not natural:
            for r in range(dil):
                src = slice(r * sub_len, (r + 1) * sub_len)
                dst = pl.ds(r, sub_len, stride=dil)
                o_br_ref[bi, dst, :] = od_ref[src, :]
                lse_br_ref[bi, dst, :] = lsed_ref[src, :]

    n_br = len(DILATED_BRANCHES)

    def merge(i, c):
        rows = pl.ds(pl.multiple_of(i * 256, 256), 256)
        lses = [lse_br_ref[b, rows, :] for b in range(n_br)]
        top = functools.reduce(jnp.maximum, lses)
        num = jnp.zeros((256, HEAD_DIM), F32)
        den = jnp.zeros((256, HEAD_DIM), F32)
        for b in range(n_br):
            e = jnp.exp(lses[b] - top)
            num = num + e * o_br_ref[b, rows, :]
            den = den + e
        o_ref[rows, :] = (num / den).astype(o_ref.dtype)
        return c
    lax.fori_loop(0, seq // 256, merge, 0)


def _attention(proj, pos_col, rope_tab, q_norm_w, k_norm_w, batch, seq, n_heads):
    t = batch * seq
    nbr = len(DILATED_BRANCHES)
    qkv_spec = lambda off: pl.BlockSpec((seq, HEAD_DIM), lambda b, h: (b, off + h))
    return pl.pallas_call(
        _attn_kernel,
        grid=(batch, n_heads),
        in_specs=[pl.BlockSpec((seq, 1), lambda b, h: (b, 0)),
                  pl.BlockSpec((2, HEAD_DIM), lambda b, h: (0, 0)),
                  pl.BlockSpec((1, HEAD_DIM), lambda b, h: (0, 0)),
                  pl.BlockSpec((1, HEAD_DIM), lambda b, h: (0, 0)),
                  qkv_spec(0), qkv_spec(n_heads), qkv_spec(2 * n_heads)],
        out_specs=pl.BlockSpec((seq, HEAD_DIM), lambda b, h: (b, h)),
        out_shape=jax.ShapeDtypeStruct((t, n_heads * HEAD_DIM), BF16),
        scratch_shapes=[
            pltpu.VMEM((seq, HEAD_DIM), F32),
            pltpu.VMEM((seq, HEAD_DIM), F32),
            pltpu.VMEM((seq, HEAD_DIM), F32),
            pltpu.VMEM((seq, HEAD_DIM), F32),
            pltpu.VMEM((seq, HEAD_DIM), BF16),
            pltpu.VMEM((ATTN_BLOCK + seq, HEAD_DIM), BF16),
            pltpu.VMEM((ATTN_BLOCK + seq, 2 * HEAD_DIM), BF16),
            pltpu.VMEM((2, ATTN_BLOCK, 2 * ATTN_BLOCK), F32),
            pltpu.VMEM((seq, 2 * ATTN_BLOCK), F32),
            pltpu.VMEM((seq, 2 * ATTN_BLOCK), BF16),
            pltpu.VMEM((seq, HEAD_DIM), F32),
            pltpu.VMEM((nbr, seq, HEAD_DIM), F32),
            pltpu.VMEM((nbr, seq, HEAD_DIM), F32),
            pltpu.VMEM((seq, HEAD_DIM), F32),
            pltpu.VMEM((seq, HEAD_DIM), F32),
        ],
        compiler_params=pltpu.CompilerParams(
            dimension_semantics=("parallel", "arbitrary"), vmem_limit_bytes=VMEM_LIMIT),
        name="dilated_attention",
    )(pos_col, rope_tab, q_norm_w.reshape(1, HEAD_DIM), k_norm_w.reshape(1, HEAD_DIM), proj, proj, proj)


def _ssd_kernel(z_ref, xbc_ref, dt_ref, convw_ref, convb_ref, dtb_ref, alog_ref, dskip_ref, normw_ref,
                o_ref, ext_ref, act_ref, state_ref, cs_ref, ecs_ref, cst_ref, wt_ref, dtt_ref, cdec_ref):
    lc = SSD_CHUNK
    width = o_ref.shape[1]
    gw = width // SSD_GROUPS
    heads_per_group = gw // SSD_HEAD_DIM
    conv_dim = xbc_ref.shape[1]
    halo = SUBLANES

    @pl.when(pl.program_id(1) == 0)
    def _():
        ext_ref[0:halo, :] = jnp.zeros((halo, conv_dim), F32)
        state_ref[...] = jnp.zeros_like(state_ref)

    ext_ref[halo:halo + lc, :] = xbc_ref[...]
    col_tile = 512
    for ct in range(conv_dim // col_tile):
        cols = slice(ct * col_tile, (ct + 1) * col_tile)
        acc = jnp.broadcast_to(convb_ref[:, cols], (lc, col_tile))
        for j in range(SSD_CONV):
            r0 = halo - (SSD_CONV - 1) + j
            acc = acc + convw_ref[j:j + 1, cols] * ext_ref[r0:r0 + lc, cols]
        act_ref[:, cols] = _silu(acc)
    ext_ref[0:halo, :] = xbc_ref[lc - halo:lc, :]

    x0 = dt_ref[...] + dtb_ref[...]
    dt = jnp.maximum(x0, 0.0) + jnp.log1p(jnp.exp(-jnp.abs(x0)))
    d_a = dt * (-jnp.exp(alog_ref[...]))
    tri = (lax.broadcasted_iota(I32, (lc, lc), 0) >= lax.broadcasted_iota(I32, (lc, lc), 1)).astype(F32)
    cs = jnp.dot(tri, d_a, precision=HIGHEST, preferred_element_type=F32)
    cs_last = cs[lc - 1:lc, :]
    cs_ref[...] = cs
    ecs_ref[...] = jnp.exp(cs)
    cdec_ref[...] = jnp.exp(cs_last)
    eye = (lax.broadcasted_iota(I32, (LANES, LANES), 0) == lax.broadcasted_iota(I32, (LANES, LANES), 1)).astype(F32)
    cst_ref[...] = _nt_dot(eye, cs, precision=HIGHEST)
    wt_ref[...] = _nt_dot(eye, jnp.exp(cs_last - cs) * dt, precision=HIGHEST)
    dtt_ref[...] = _nt_dot(eye, dt, precision=HIGHEST)

    causal = lax.broadcasted_iota(I32, (lc, lc), 0) >= lax.broadcasted_iota(I32, (lc, lc), 1)
    lane_head = lax.broadcasted_iota(I32, (lc, gw), 1) // SSD_HEAD_DIM
    lane_head_s = lax.broadcasted_iota(I32, (SSD_STATE, gw), 1) // SSD_HEAD_DIM
    n_xs = width
    n_b = SSD_GROUPS * SSD_STATE

    for g in range(SSD_GROUPS):
        gc = slice(g * gw, (g + 1) * gw)
        xs = act_ref[:, gc]
        b_g = act_ref[:, n_xs + g * SSD_STATE:n_xs + (g + 1) * SSD_STATE]
        c_g = act_ref[:, n_xs + n_b + g * SSD_STATE:n_xs + n_b + (g + 1) * SSD_STATE]
        xs_b = xs.astype(BF16)
        c_b = c_g.astype(BF16)
        cb = _nt_dot(c_b, b_g.astype(BF16))
        b_t = jnp.transpose(b_g)
        s_prev = state_ref[g]
        y_off = jnp.dot(c_b, s_prev.astype(BF16), preferred_element_type=F32)
        y = jnp.zeros((lc, gw), F32)
        s_new = jnp.zeros((SSD_STATE, gw), F32)
        for r in range(heads_per_group):
            h = g * heads_per_group + r
            diff = cs_ref[:, h:h + 1] - cst_ref[h:h + 1, :]
            seg = jnp.exp(jnp.where(causal, diff, -jnp.inf))
            m_h = (cb * seg * dtt_ref[h:h + 1, :]).astype(BF16)
            y_h = jnp.dot(m_h, xs_b, preferred_element_type=F32)
            bw = (b_t * wt_ref[h:h + 1, :]).astype(BF16)
            s_h = jnp.dot(bw, xs_b, preferred_element_type=F32)
            y = jnp.where(lane_head == r, y_h + ecs_ref[:, h:h + 1] * y_off, y)
            s_new = jnp.where(lane_head_s == r, cdec_ref[:, h:h + 1] * s_prev + s_h, s_new)
        state_ref[g] = s_new
        y = y + dskip_ref[:, gc] * xs
        y = y * _silu(z_ref[:, gc])
        y = y * lax.rsqrt(jnp.mean(y * y, axis=-1, keepdims=True) + SSD_NORM_EPS)
        o_ref[:, gc] = (y * normw_ref[:, gc]).astype(o_ref.dtype)


def _ssd(proj, dt_raw, conv_w, conv_b, dt_bias_pad, a_log_pad, d_skip_ch, norm_w, batch, seq, z_col, xbc_col, width):
    t = batch * seq
    nc = seq // SSD_CHUNK
    conv_dim = conv_w.shape[1]
    gw = width // SSD_GROUPS
    row = lambda b, c: b * nc + c
    return pl.pallas_call(
        _ssd_kernel,
        grid=(batch, nc),
        in_specs=[pl.BlockSpec((SSD_CHUNK, width), lambda b, c: (row(b, c), z_col)),
                  pl.BlockSpec((SSD_CHUNK, conv_dim), lambda b, c: (row(b, c), xbc_col)),
                  pl.BlockSpec((SSD_CHUNK, LANES), lambda b, c: (row(b, c), 0)),
                  pl.BlockSpec((SSD_CONV, conv_dim), lambda b, c: (0, 0)),
                  pl.BlockSpec((1, conv_dim), lambda b, c: (0, 0)),
                  pl.BlockSpec((1, LANES), lambda b, c: (0, 0)),
                  pl.BlockSpec((1, LANES), lambda b, c: (0, 0)),
                  pl.BlockSpec((1, width), lambda b, c: (0, 0)),
                  pl.BlockSpec((1, width), lambda b, c: (0, 0))],
        out_specs=pl.BlockSpec((SSD_CHUNK, width), lambda b, c: (row(b, c), 0)),
        out_shape=jax.ShapeDtypeStruct((t, width), BF16),
        scratch_shapes=[
            pltpu.VMEM((SUBLANES + SSD_CHUNK, conv_dim), F32),
            pltpu.VMEM((SSD_CHUNK, conv_dim), F32),
            pltpu.VMEM((SSD_GROUPS, SSD_STATE, gw), F32),
            pltpu.VMEM((SSD_CHUNK, LANES), F32),
            pltpu.VMEM((SSD_CHUNK, LANES), F32),
            pltpu.VMEM((LANES, SSD_CHUNK), F32),
            pltpu.VMEM((LANES, SSD_CHUNK), F32),
            pltpu.VMEM((LANES, SSD_CHUNK), F32),
            pltpu.VMEM((1, LANES), F32),
        ],
        compiler_params=pltpu.CompilerParams(
            dimension_semantics=("parallel", "arbitrary"), vmem_limit_bytes=VMEM_LIMIT),
        name="ssd_mixer",
    )(proj, proj, dt_raw, conv_w, conv_b.reshape(1, conv_dim), dt_bias_pad, a_log_pad,
      d_skip_ch.reshape(1, width), norm_w.reshape(1, width))


def _router_kernel(x_ref, nw_ref, wr_ref, br_ref, h_ref, meta_ref, cnt_ref, carry_ref):
    tm = x_ref.shape[0]

    @pl.when(pl.program_id(0) == 0)
    def _():
        carry_ref[...] = jnp.zeros_like(carry_ref)

    x = x_ref[...]
    h = x * lax.rsqrt(jnp.mean(x * x, axis=-1, keepdims=True) + NORM_EPS) * nw_ref[...]
    h_ref[...] = h

    logits = jnp.dot(h, wr_ref[...], precision=HIGHEST, preferred_element_type=F32) + br_ref[...]
    lane = lax.broadcasted_iota(I32, (tm, LANES), 1)
    neg_inf = jnp.float32(-jnp.inf)

    lane_f = lane.astype(F32)

    def first_argmax(v, vmax):
        return jnp.min(jnp.where(v == vmax, lane_f, float(LANES)), axis=1, keepdims=True).astype(I32)

    lg = jnp.where(lane < N_EXPERT_GROUPS, logits, neg_inf)
    g_max = jnp.max(lg, axis=1, keepdims=True)
    g_idx = first_argmax(lg, g_max)
    g_p = 1.0 / jnp.sum(jnp.exp(lg - g_max), axis=1, keepdims=True)

    e_lo = ROUTER_EXPERT_LANE0 + EXPERTS_PER_GROUP * g_idx
    le = jnp.where((lane >= e_lo) & (lane < e_lo + EXPERTS_PER_GROUP), logits, neg_inf)
    e_max = jnp.max(le, axis=1, keepdims=True)
    i1 = first_argmax(le, e_max)
    e_sum = jnp.sum(jnp.exp(le - e_max), axis=1, keepdims=True)
    le2 = jnp.where(lane == i1, neg_inf, le)
    e_max2 = jnp.max(le2, axis=1, keepdims=True)
    i2 = first_argmax(le2, e_max2)
    p1 = 1.0 / e_sum
    p2 = jnp.exp(e_max2 - e_max) / e_sum
    gate1 = g_p * (p1 / (p1 + p2))
    gate2 = g_p * (p2 / (p1 + p2))

    hot = ((lane == i1) | (lane == i2))
    hot_b = hot.astype(BF16)
    strict = (lax.broadcasted_iota(I32, (tm, tm), 0) > lax.broadcasted_iota(I32, (tm, tm), 1)).astype(BF16)
    before = jnp.dot(strict, hot_b, preferred_element_type=F32) + carry_ref[0:1, :]
    rank1 = jnp.sum(jnp.where(lane == i1, before, 0.0), axis=1, keepdims=True)
    rank2 = jnp.sum(jnp.where(lane == i2, before, 0.0), axis=1, keepdims=True)
    carry_ref[0:1, :] = carry_ref[0:1, :] + jnp.sum(hot.astype(F32), axis=0, keepdims=True)
    cnt_ref[...] = jnp.broadcast_to(carry_ref[0:1, :], cnt_ref.shape)

    vals = [(i1 - ROUTER_EXPERT_LANE0).astype(F32), (i2 - ROUTER_EXPERT_LANE0).astype(F32), rank1, rank2, gate1, gate2]
    meta = jnp.zeros((tm, LANES), F32)
    for c, v in enumerate(vals):
        meta = jnp.where(lane == c, v, meta)
    meta_ref[...] = meta


def _router(x1, norm_w, w_router, b_router):
    t, d = x1.shape
    return pl.pallas_call(
        _router_kernel,
        grid=(t // ROUTER_TILE,),
        in_specs=[pl.BlockSpec((ROUTER_TILE, d), lambda i: (i, 0)),
                  pl.BlockSpec((1, d), lambda i: (0, 0)),
                  pl.BlockSpec((d, LANES), lambda i: (0, 0)),
                  pl.BlockSpec((1, LANES), lambda i: (0, 0))],
        out_specs=[pl.BlockSpec((ROUTER_TILE, d), lambda i: (i, 0)),
                   pl.BlockSpec((ROUTER_TILE, LANES), lambda i: (i, 0)),
                   pl.BlockSpec((SUBLANES, LANES), lambda i: (0, 0))],
        out_shape=[jax.ShapeDtypeStruct((t, d), F32),
                   jax.ShapeDtypeStruct((t, LANES), F32),
                   jax.ShapeDtypeStruct((SUBLANES, LANES), F32)],
        scratch_shapes=[pltpu.VMEM((SUBLANES, LANES), F32)],
        compiler_params=pltpu.CompilerParams(dimension_semantics=("arbitrary",), vmem_limit_bytes=VMEM_LIMIT),
        name="router",
    )(x1, norm_w.reshape(1, d), w_router, b_router)


def _row_copy(src_ref, dst_ref, src_row, dst_row, sem):
    return pltpu.make_async_copy(src_ref.at[pl.ds(src_row, 1)], dst_ref.at[pl.ds(dst_row, 1)], sem)


def _wait_rows(src_ref, dst_ref, n, sem):
    done = 0
    for rows in (64, SUBLANES, 1):
        trips = lax.shift_right_logical(n - done, rows.bit_length() - 1)

        def body(r, c, rows=rows):
            pltpu.make_async_copy(src_ref.at[pl.ds(0, rows)], dst_ref.at[pl.ds(0, rows)], sem).wait()
            return c
        lax.fori_loop(0, trips, body, 0)
        done = done + trips * rows


def _for_rows(n, body):
    n_groups = lax.shift_right_logical(n, ROW_UNROLL.bit_length() - 1)

    def group(g, c):
        for u in range(ROW_UNROLL):
            body(g * ROW_UNROLL + u)
        return c
    lax.fori_loop(0, n_groups, group, 0)

    def tail(r, c):
        body(r)
        return c
    lax.fori_loop(n_groups * ROW_UNROLL, n, tail, 0)


def _for_row_blocks(n_rows, fn):
    n_blk = lax.shift_right_logical(n_rows + (MOE_SUB - 1), MOE_SUB.bit_length() - 1)
    for blocks in range(1, MOE_ROWS // MOE_SUB + 1):
        pl.when(n_blk == blocks)(functools.partial(fn, blocks * MOE_SUB))


def _moe_up_kernel(n_items_ref, item_e_ref, item_start_ref, item_n_ref, flat_ref,
                   h_hbm, wg_ref, wu_ref, o_ref, xbuf_ref, xb_ref, sem, *, n_tokens):
    i = pl.program_id(0)
    j = pl.program_id(1)
    n_items = n_items_ref[0]
    valid = i < n_items
    conv_rows = 64

    def start_gather(item):
        start = item_start_ref[item]

        def body(r):
            tok = flat_ref[start + r] & (n_tokens - 1)
            _row_copy(h_hbm, xbuf_ref, tok, r, sem).start()
        _for_rows(item_n_ref[item], body)

    @pl.when((i == 0) & (j == 0))
    def _():
        xbuf_ref[...] = jnp.zeros_like(xbuf_ref)
        start_gather(0)

    @pl.when(valid & (j == 0))
    def _():
        _wait_rows(h_hbm, xbuf_ref, item_n_ref[i], sem)

        def convert(s, c):
            rows = pl.ds(pl.multiple_of(s * conv_rows, conv_rows), conv_rows)
            xb_ref[rows, :] = xbuf_ref[rows, :].astype(BF16)
            return c
        lax.fori_loop(0, MOE_ROWS // conv_rows, convert, 0)

        @pl.when(i + 1 < n_items)
        def _():
            start_gather(i + 1)

    o_ref[...] = jnp.zeros_like(o_ref)

    def compute(m):
        xs = xb_ref[0:m, :]
        g = jnp.dot(xs, wg_ref[...].astype(BF16), preferred_element_type=F32)
        u = jnp.dot(xs, wu_ref[...].astype(BF16), preferred_element_type=F32)
        o_ref[0:m, :] = (_silu(g) * u).astype(o_ref.dtype)

    @pl.when(valid)
    def _():
        _for_row_blocks(item_n_ref[i], compute)


def _moe_up(sched, h2, w_gate, w_up, n_items_max):
    n_tokens, _ = h2.shape
    _, d, ff = w_gate.shape
    nj = ff // MOE_FF_TILE

    def w_map(i, j, n_items, item_e, *_):
        ii = jnp.minimum(i, n_items[0] - 1)
        return (item_e[ii], 0, jnp.where(i < n_items[0], j, nj - 1))

    def o_map(i, j, *_):
        return (i, j)

    return pl.pallas_call(
        functools.partial(_moe_up_kernel, n_tokens=n_tokens),
        grid_spec=pltpu.PrefetchScalarGridSpec(
            num_scalar_prefetch=5,
            grid=(n_items_max, nj),
            in_specs=[pl.BlockSpec(memory_space=pl.ANY),
                      pl.BlockSpec((None, d, MOE_FF_TILE), w_map),
                      pl.BlockSpec((None, d, MOE_FF_TILE), w_map)],
            out_specs=pl.BlockSpec((MOE_ROWS, MOE_FF_TILE), o_map),
            scratch_shapes=[pltpu.VMEM((MOE_ROWS, d), F32),
                            pltpu.VMEM((MOE_ROWS, d), BF16),
                            pltpu.SemaphoreType.DMA(())]),
        out_shape=jax.ShapeDtypeStruct((n_items_max * MOE_ROWS, ff), BF16),
        compiler_params=pltpu.CompilerParams(
            dimension_semantics=("arbitrary", "arbitrary"), vmem_limit_bytes=VMEM_LIMIT),
        name="moe_up",
    )(*sched, h2, w_gate, w_up)


def _moe_down_kernel(n_items_ref, item_e_ref, item_start_ref, item_n_ref, flat_ref,
                     h_ref, wd_ref, y_hbm, *scratch):
    i = pl.program_id(0)
    j = pl.program_id(1)
    valid = i < n_items_ref[0]
    tn = wd_ref.shape[1]
    ybufs = scratch[:MOE_DOWN_SPLIT]
    sems = scratch[MOE_DOWN_SPLIT:2 * MOE_DOWN_SPLIT]
    pending_ref = scratch[2 * MOE_DOWN_SPLIT]

    @pl.when((i == 0) & (j == 0))
    def _():
        for jj in range(MOE_DOWN_SPLIT):
            pending_ref[jj] = 0

    def drain(jj):
        y_part = y_hbm.at[:, pl.ds(jj * tn, tn)]
        _wait_rows(ybufs[jj], y_part, pending_ref[jj], sems[jj])
        pending_ref[jj] = 0

    for jj in range(MOE_DOWN_SPLIT):
        @pl.when(valid & (j == jj))
        def _(jj=jj):
            ybuf = ybufs[jj]
            y_part = y_hbm.at[:, pl.ds(jj * tn, tn)]
            drain(jj)
            n = item_n_ref[i]
            start = item_start_ref[i]

            def compute(m):
                ybuf[0:m, :] = jnp.dot(h_ref[0:m, :], wd_ref[...].astype(BF16), preferred_element_type=F32)
            _for_row_blocks(n, compute)

            def issue(r):
                _row_copy(ybuf, y_part, r, flat_ref[start + r], sems[jj]).start()
            _for_rows(n, issue)
            pending_ref[jj] = n

    @pl.when((i == pl.num_programs(0) - 1) & (j == MOE_DOWN_SPLIT - 1))
    def _():
        for jj in range(MOE_DOWN_SPLIT):
            drain(jj)


def _moe_down(sched, h_items, w_down, n_items_max, n_tokens):
    _, ff, d = w_down.shape
    tn = d // MOE_DOWN_SPLIT

    def w_map(i, j, n_items, item_e, *_):
        ii = jnp.minimum(i, n_items[0] - 1)
        return (item_e[ii], 0, jnp.where(i < n_items[0], j, MOE_DOWN_SPLIT - 1))

    def h_map(i, j, n_items, *_):
        return (jnp.minimum(i, n_items[0] - 1), 0)

    return pl.pallas_call(
        _moe_down_kernel,
        grid_spec=pltpu.PrefetchScalarGridSpec(
            num_scalar_prefetch=5,
            grid=(n_items_max, MOE_DOWN_SPLIT),
            in_specs=[pl.BlockSpec((MOE_ROWS, ff), h_map),
                      pl.BlockSpec((None, ff, tn), w_map)],
            out_specs=pl.BlockSpec(memory_space=pl.ANY),
            scratch_shapes=([pltpu.VMEM((MOE_ROWS, tn), F32)] * MOE_DOWN_SPLIT
                            + [pltpu.SemaphoreType.DMA(())] * MOE_DOWN_SPLIT
                            + [pltpu.SMEM((MOE_DOWN_SPLIT,), I32)])),
        out_shape=jax.ShapeDtypeStruct((TOP_K * n_tokens, d), F32),
        compiler_params=pltpu.CompilerParams(
            dimension_semantics=("arbitrary", "arbitrary"), vmem_limit_bytes=VMEM_LIMIT),
        name="moe_down",
    )(*sched, h_items, w_down)


def _combine_kernel(x_ref, meta_ref, y0_ref, y1_ref, o_ref):
    o_ref[...] = x_ref[...] + meta_ref[:, 4:5] * y0_ref[...] + meta_ref[:, 5:6] * y1_ref[...]


def _combine(x1, meta, y_slots):
    t, d = x1.shape
    tm = ROUTER_TILE
    nt = t // tm
    return pl.pallas_call(
        _combine_kernel,
        grid=(nt,),
        in_specs=[pl.BlockSpec((tm, d), lambda i: (i, 0)),
                  pl.BlockSpec((tm, LANES), lambda i: (i, 0)),
                  pl.BlockSpec((tm, d), lambda i: (i, 0)),
                  pl.BlockSpec((tm, d), lambda i: (nt + i, 0))],
        out_specs=pl.BlockSpec((tm, d), lambda i: (i, 0)),
        out_shape=jax.ShapeDtypeStruct((t, d), F32),
        compiler_params=pltpu.CompilerParams(dimension_semantics=("parallel",), vmem_limit_bytes=VMEM_LIMIT),
        name="moe_combine",
    )(x1, meta, y_slots, y_slots)


def _moe_schedule(meta, counts_f, n_tokens, n_items_max):
    e_id = meta[:, 0:TOP_K].astype(I32)
    rank = meta[:, TOP_K:2 * TOP_K].astype(I32)
    counts = counts_f[0, ROUTER_EXPERT_LANE0:ROUTER_EXPERT_LANE0 + N_EXPERTS].astype(I32)
    starts = jnp.cumsum(counts) - counts
    pos = starts[e_id] + rank
    flat = jnp.arange(TOP_K, dtype=I32)[None, :] * n_tokens + jnp.arange(n_tokens, dtype=I32)[:, None]
    flat_sorted = jnp.zeros((TOP_K * n_tokens,), I32).at[pos.reshape(-1)].set(flat.reshape(-1))
    chunks = (counts + MOE_ROWS - 1) // MOE_ROWS
    chunk_end = jnp.cumsum(chunks)
    n_items = chunk_end[-1]
    item = jnp.arange(n_items_max, dtype=I32)
    item_e = jnp.minimum(jnp.sum(item[:, None] >= chunk_end[None, :], axis=1), N_EXPERTS - 1).astype(I32)
    local = item - (chunk_end - chunks)[item_e]
    item_start = starts[item_e] + local * MOE_ROWS
    item_n = jnp.clip(counts[item_e] - local * MOE_ROWS, 0, MOE_ROWS)
    in_range = item < n_items
    item_start = jnp.where(in_range, item_start, 0).astype(I32)
    item_n = jnp.where(in_range, item_n, 0).astype(I32)
    return (n_items.reshape(1).astype(I32), item_e, item_start, item_n, flat_sorted)


def _layer(x, positions, norm_attn_w, w_in, q_norm_w, k_norm_w, conv_w, conv_b, dt_bias, a_log, d_skip,
           ssd_norm_w, w_out, norm_ffn_w, router_group_w, router_group_b, router_expert_w, router_expert_b,
           w_gate, w_up, w_down):
    batch, seq, d = x.shape
    t = batch * seq
    attn_width = d // 2
    n_heads = attn_width // HEAD_DIM
    ssd_width = d - attn_width
    ssd_heads = ssd_width // SSD_HEAD_DIM
    conv_dim = ssd_width + 2 * SSD_GROUPS * SSD_STATE
    main_cols = 3 * attn_width + ssd_width + conv_dim
    assert w_in.shape[1] == main_cols + ssd_heads and ssd_heads <= LANES
    assert seq % (SSD_CHUNK) == 0 and t % IN_TM == 0 and (t & (t - 1)) == 0

    x2d = x.reshape(t, d)
    h = _rmsnorm_cast(x2d, norm_attn_w)
    wt_in = jnp.transpose(w_in).astype(BF16)
    proj = _matmul_nt(h, wt_in, main_cols, IN_TM, IN_TN, "in_proj")
    wt_dt = jnp.pad(wt_in[main_cols:, :], ((0, LANES - ssd_heads), (0, 0)))
    dt_raw = _matmul_nt(h, wt_dt, LANES, IN_TM, LANES, "dt_proj")

    half = HEAD_DIM // 2
    inv_freq = jnp.power(jnp.float32(ROPE_THETA), -jnp.arange(half, dtype=F32) / half)
    rope_tab = jnp.stack([jnp.concatenate([inv_freq, inv_freq]),
                          jnp.concatenate([-jnp.ones((half,), F32), jnp.ones((half,), F32)])])
    attn = _attention(proj, positions.reshape(t, 1), rope_tab, q_norm_w, k_norm_w, batch, seq, n_heads)

    pad_heads = lambda v: jnp.pad(v.astype(F32), (0, LANES - ssd_heads)).reshape(1, LANES)
    ssd = _ssd(proj, dt_raw, conv_w, conv_b, pad_heads(dt_bias), pad_heads(a_log),
               jnp.repeat(d_skip.astype(F32), SSD_HEAD_DIM), ssd_norm_w, batch, seq,
               z_col=(3 * attn_width) // ssd_width, xbc_col=(3 * attn_width + ssd_width) // conv_dim,
               width=ssd_width)

    x1 = _out_proj(attn, ssd, w_out.astype(BF16), x2d, OUT_TM, OUT_TN)

    n_router = N_EXPERT_GROUPS + N_EXPERTS
    w_router = jnp.pad(jnp.concatenate([router_group_w, router_expert_w], axis=1), ((0, 0), (0, LANES - n_router)))
    b_router = jnp.pad(jnp.concatenate([router_group_b, router_expert_b]), (0, LANES - n_router)).reshape(1, LANES)
    h2, meta, counts_f = _router(x1, norm_ffn_w, w_router, b_router)

    n_items_max = N_EXPERTS + (TOP_K * t) // MOE_ROWS
    sched = _moe_schedule(meta, counts_f, t, n_items_max)
    h_items = _moe_up(sched, h2, w_gate, w_up, n_items_max)
    y_slots = _moe_down(sched, h_items, w_down, n_items_max, t)
    out = _combine(x1, meta, y_slots)
    return out.reshape(batch, seq, d)


def kernel(x, positions, norm_attn_w, w_in, q_norm_w, k_norm_w, conv_w, conv_b, dt_bias, a_log, d_skip, ssd_norm_w, w_out, norm_ffn_w, router_group_w, router_group_b, router_expert_w, router_expert_b, w_gate, w_up, w_down):
    for layer in range(norm_attn_w.shape[0]):
        x = _layer(x, positions, norm_attn_w[layer], w_in[layer], q_norm_w[layer], k_norm_w[layer],
                   conv_w[layer], conv_b[layer], dt_bias[layer], a_log[layer], d_skip[layer],
                   ssd_norm_w[layer], w_out[layer], norm_ffn_w[layer], router_group_w[layer],
                   router_group_b[layer], router_expert_w[layer], router_expert_b[layer],
                   w_gate[layer], w_up[layer], w_down[layer])
    return x
```

```python
import functools

import jax
import jax.numpy as jnp
from jax import lax
from jax.experimental import pallas as pl
from jax.experimental.pallas import tpu as pltpu

F32 = jnp.float32
BF16 = jnp.bfloat16
I32 = jnp.int32
HIGHEST = lax.Precision.HIGHEST

HEAD_DIM = 128
DILATED_BRANCHES = ((128, 1), (512, 4), (2048, 16))
ATTN_BLOCK = 128
ROPE_THETA = 10000.0
SSD_HEAD_DIM = 64
SSD_GROUPS = 8
SSD_STATE = 128
SSD_CONV = 4
SSD_CHUNK = 256
N_EXPERT_GROUPS = 4
EXPERTS_PER_GROUP = 8
N_EXPERTS = N_EXPERT_GROUPS * EXPERTS_PER_GROUP
TOP_K = 2
NORM_EPS = 1e-6
SSD_NORM_EPS = 1e-5

LANES = 128
SUBLANES = 8
VMEM_LIMIT = 56 * 1024 * 1024

ROW_TILE = 512
IN_TM, IN_TN = 1024, 1024
OUT_TM, OUT_TN = 1024, 512
MOE_ROWS = 1024
MOE_SUB = 256
ROW_UNROLL = 8
MOE_FF_TILE = 256
MOE_DOWN_SPLIT = 2
ROUTER_TILE = 256
ROUTER_EXPERT_LANE0 = N_EXPERT_GROUPS
ROUTER_PART_LANES = 40


def _silu(v):
    return v * (1.0 / (1.0 + jnp.exp(-v)))


def _nt_dot(a, b, **kw):
    return lax.dot_general(a, b, (((1,), (1,)), ((), ())), preferred_element_type=F32, **kw)


def _rmsnorm_cast_kernel(x_ref, w_ref, o_ref):
    x = x_ref[...]
    y = x * lax.rsqrt(jnp.mean(x * x, axis=-1, keepdims=True) + NORM_EPS)
    o_ref[...] = (y * w_ref[...]).astype(o_ref.dtype)


def _rmsnorm_cast(x2d, w):
    t, d = x2d.shape
    return pl.pallas_call(
        _rmsnorm_cast_kernel,
        grid=(t // ROW_TILE,),
        in_specs=[pl.BlockSpec((ROW_TILE, d), lambda i: (i, 0)),
                  pl.BlockSpec((1, d), lambda i: (0, 0))],
        out_specs=pl.BlockSpec((ROW_TILE, d), lambda i: (i, 0)),
        out_shape=jax.ShapeDtypeStruct((t, d), BF16),
        compiler_params=pltpu.CompilerParams(dimension_semantics=("parallel",), vmem_limit_bytes=VMEM_LIMIT),
        name="rmsnorm_cast",
    )(x2d, w.reshape(1, d))


def _matmul_nt_kernel(a_ref, wt_ref, o_ref):
    o_ref[...] = _nt_dot(a_ref[...], wt_ref[...]).astype(o_ref.dtype)


def _matmul_nt(a, wt, n_rows, tm, tn, name):
    m, kdim = a.shape
    return pl.pallas_call(
        _matmul_nt_kernel,
        grid=(m // tm, n_rows // tn),
        in_specs=[pl.BlockSpec((tm, kdim), lambda i, j: (i, 0)),
                  pl.BlockSpec((tn, kdim), lambda i, j: (j, 0))],
        out_specs=pl.BlockSpec((tm, tn), lambda i, j: (i, j)),
        out_shape=jax.ShapeDtypeStruct((m, n_rows), F32),
        compiler_params=pltpu.CompilerParams(
            dimension_semantics=("parallel", "parallel"), vmem_limit_bytes=VMEM_LIMIT),
        name=name,
    )(a, wt)


def _out_proj_kernel(a1_ref, a2_ref, w_ref, x_ref, o_ref):
    k1 = a1_ref.shape[1]
    acc = jnp.dot(a1_ref[...], w_ref[0:k1, :], preferred_element_type=F32)
    acc = acc + jnp.dot(a2_ref[...], w_ref[k1:, :], preferred_element_type=F32)
    o_ref[...] = x_ref[...] + acc


def _out_proj(a1, a2, w, x2d, tm, tn):
    m, k1 = a1.shape
    k2 = a2.shape[1]
    kdim, n = w.shape
    return pl.pallas_call(
        _out_proj_kernel,
        grid=(m // tm, n // tn),
        in_specs=[pl.BlockSpec((tm, k1), lambda i, j: (i, 0)),
                  pl.BlockSpec((tm, k2), lambda i, j: (i, 0)),
                  pl.BlockSpec((kdim, tn), lambda i, j: (0, j)),
                  pl.BlockSpec((tm, tn), lambda i, j: (i, j))],
        out_specs=pl.BlockSpec((tm, tn), lambda i, j: (i, j)),
        out_shape=jax.ShapeDtypeStruct((m, n), F32),
        compiler_params=pltpu.CompilerParams(
            dimension_semantics=("parallel", "parallel"), vmem_limit_bytes=VMEM_LIMIT),
        name="out_proj",
    )(a1, a2, w, x2d)


def _attn_kernel(pos_ref, rope_ref, qw_ref, kw_ref, q_ref, k_ref, v_ref, o_ref,
                 cos_ref, sin_ref, qf_ref, kf_ref, qd_ref, kd_ref, vd_ref,
                 bias_ref, s_ref, p_ref, m_ref, o_br_ref, lse_br_ref, od_ref, lsed_ref):
    seq = q_ref.shape[0]
    blk = ATTN_BLOCK
    n_blocks = seq // blk

    @pl.when(pl.program_id(1) == 0)
    def _():
        ang = pos_ref[...].astype(F32) * rope_ref[0:1, :]
        cos_ref[...] = jnp.cos(ang)
        sin_ref[...] = jnp.sin(ang) * rope_ref[1:2, :]

    def norm_rot(t_ref, w_ref, dst_ref, scale):
        def body(i, c):
            rows = pl.ds(pl.multiple_of(i * 256, 256), 256)
            t = t_ref[rows, :]
            y = t * lax.rsqrt(jnp.mean(t * t, axis=-1, keepdims=True) + NORM_EPS) * w_ref[...]
            y = y * cos_ref[rows, :] + pltpu.roll(y, HEAD_DIM // 2, 1) * sin_ref[rows, :]
            dst_ref[rows, :] = y * scale if scale != 1.0 else y
            return c
        lax.fori_loop(0, seq // 256, body, 0, unroll=2)

    norm_rot(q_ref, qw_ref, qf_ref, HEAD_DIM ** -0.5)
    norm_rot(k_ref, kw_ref, kf_ref, 1.0)

    kd_ref[0:blk, :] = jnp.zeros((blk, HEAD_DIM), BF16)
    vd_ref[0:blk, 0:HEAD_DIM] = jnp.zeros((blk, HEAD_DIM), BF16)
    vd_ref[:, HEAD_DIM:] = jnp.ones((blk + seq, HEAD_DIM), BF16)

    for bi, (window, dil) in enumerate(DILATED_BRANCHES):
        n_back = window // dil
        sub_len = seq // dil
        nb = sub_len // blk
        natural = dil == 1
        kw = blk if nb == 1 else 2 * blk
        k_off = blk if nb == 1 else 0
        for r in range(dil):
            src = slice(None) if natural else pl.ds(r, sub_len, stride=dil)
            dst = slice(r * sub_len, (r + 1) * sub_len)
            dstp = slice(blk + r * sub_len, blk + (r + 1) * sub_len)
            qd_ref[dst, :] = qf_ref[src, :].astype(BF16)
            kd_ref[dstp, :] = kf_ref[src, :].astype(BF16)
            vd_ref[dstp, 0:HEAD_DIM] = v_ref[src, :].astype(BF16)

        o_out = o_br_ref.at[bi] if natural else od_ref
        lse_out = lse_br_ref.at[bi] if natural else lsed_ref

        qi = lax.broadcasted_iota(I32, (blk, kw), 0)
        kj = lax.broadcasted_iota(I32, (blk, kw), 1)
        dist = (blk - k_off) + qi - kj
        ok = (dist >= 0) & (dist <= n_back)
        bias_ref[0, :, 0:kw] = jnp.where(ok, 0.0, -jnp.inf)
        bias_ref[1, :, 0:kw] = jnp.where(ok & (kj >= blk - k_off), 0.0, -jnp.inf)

        def scores(j, c):
            r0 = pl.multiple_of(j * blk, blk)
            s_ref[pl.ds(r0, blk), 0:kw] = _nt_dot(qd_ref[pl.ds(r0, blk), :], kd_ref[pl.ds(r0 + k_off, kw), :])
            return c
        lax.fori_loop(0, n_blocks, scores, 0, unroll=8)

        def softmax(j, c):
            r0 = pl.multiple_of(j * blk, blk)
            first = jnp.where((j & (nb - 1)) == 0, 1, 0)
            s = s_ref[pl.ds(r0, blk), 0:kw] + bias_ref[first, :, 0:kw]
            m = jnp.max(s, axis=1, keepdims=True)
            p_ref[pl.ds(r0, blk), 0:kw] = jnp.exp(s - m).astype(BF16)
            m_ref[pl.ds(r0, blk), :] = jnp.broadcast_to(m, (blk, HEAD_DIM))
            return c
        lax.fori_loop(0, n_blocks, softmax, 0, unroll=4)

        def weighted(j, c):
            r0 = pl.multiple_of(j * blk, blk)
            acc = jnp.dot(p_ref[pl.ds(r0, blk), 0:kw], vd_ref[pl.ds(r0 + k_off, kw), :], preferred_element_type=F32)
            l = acc[:, HEAD_DIM:]
            o_out[pl.ds(r0, blk), :] = acc[:, 0:HEAD_DIM] / l
            lse_out[pl.ds(r0, blk), :] = m_ref[pl.ds(r0, blk), :] + jnp.log(l)
            return c
        lax.fori_loop(0, n_blocks, weighted, 0, unroll=8)

        if not natural:
            for r in range(dil):
                src = slice(r * sub_len, (r + 1) * sub_len)
                dst = pl.ds(r, sub_len, stride=dil)
                o_br_ref[bi, dst, :] = od_ref[src, :]
                lse_br_ref[bi, dst, :] = lsed_ref[src, :]

    n_br = len(DILATED_BRANCHES)

    def merge(i, c):
        rows = pl.ds(pl.multiple_of(i * 256, 256), 256)
        lses = [lse_br_ref[b, rows, :] for b in range(n_br)]
        top = functools.reduce(jnp.maximum, lses)
        num = jnp.zeros((256, HEAD_DIM), F32)
        den = jnp.zeros((256, HEAD_DIM), F32)
        for b in range(n_br):
            e = jnp.exp(lses[b] - top)
            num = num + e * o_br_ref[b, rows, :]
            den = den + e
        o_ref[rows, :] = (num / den).astype(o_ref.dtype)
        return c
    lax.fori_loop(0, seq // 256, merge, 0)


def _attention(proj, pos_col, rope_tab, q_norm_w, k_norm_w, batch, seq, n_heads):
    t = batch * seq
    nbr = len(DILATED_BRANCHES)
    qkv_spec = lambda off: pl.BlockSpec((seq, HEAD_DIM), lambda b, h: (b, off + h))
    return pl.pallas_call(
        _attn_kernel,
        grid=(batch, n_heads),
        in_specs=[pl.BlockSpec((seq, 1), lambda b, h: (b, 0)),
                  pl.BlockSpec((2, HEAD_DIM), lambda b, h: (0, 0)),
                  pl.BlockSpec((1, HEAD_DIM), lambda b, h: (0, 0)),
                  pl.BlockSpec((1, HEAD_DIM), lambda b, h: (0, 0)),
                  qkv_spec(0), qkv_spec(n_heads), qkv_spec(2 * n_heads)],
        out_specs=pl.BlockSpec((seq, HEAD_DIM), lambda b, h: (b, h)),
        out_shape=jax.ShapeDtypeStruct((t, n_heads * HEAD_DIM), BF16),
        scratch_shapes=[
            pltpu.VMEM((seq, HEAD_DIM), F32),
            pltpu.VMEM((seq, HEAD_DIM), F32),
            pltpu.VMEM((seq, HEAD_DIM), F32),
            pltpu.VMEM((seq, HEAD_DIM), F32),
            pltpu.VMEM((seq, HEAD_DIM), BF16),
            pltpu.VMEM((ATTN_BLOCK + seq, HEAD_DIM), BF16),
            pltpu.VMEM((ATTN_BLOCK + seq, 2 * HEAD_DIM), BF16),
            pltpu.VMEM((2, ATTN_BLOCK, 2 * ATTN_BLOCK), F32),
            pltpu.VMEM((seq, 2 * ATTN_BLOCK), F32),
            pltpu.VMEM((seq, 2 * ATTN_BLOCK), BF16),
            pltpu.VMEM((seq, HEAD_DIM), F32),
            pltpu.VMEM((nbr, seq, HEAD_DIM), F32),
            pltpu.VMEM((nbr, seq, HEAD_DIM), F32),
            pltpu.VMEM((seq, HEAD_DIM), F32),
            pltpu.VMEM((seq, HEAD_DIM), F32),
        ],
        compiler_params=pltpu.CompilerParams(
            dimension_semantics=("parallel", "arbitrary"), vmem_limit_bytes=VMEM_LIMIT),
        name="dilated_attention",
    )(pos_col, rope_tab, q_norm_w.reshape(1, HEAD_DIM), k_norm_w.reshape(1, HEAD_DIM), proj, proj, proj)


def _ssd_kernel(z_ref, xbc_ref, dt_ref, convw_ref, convb_ref, dtb_ref, alog_ref, dskip_ref, normw_ref,
                o_ref, ext_ref, act_ref, state_ref, cs_ref, ecs_ref, cst_ref, wt_ref, dtt_ref, cdec_ref):
    lc = SSD_CHUNK
    width = o_ref.shape[1]
    gw = width // SSD_GROUPS
    heads_per_group = gw // SSD_HEAD_DIM
    conv_dim = xbc_ref.shape[1]
    halo = SUBLANES

    @pl.when(pl.program_id(1) == 0)
    def _():
        ext_ref[0:halo, :] = jnp.zeros((halo, conv_dim), F32)
        state_ref[...] = jnp.zeros_like(state_ref)

    ext_ref[halo:halo + lc, :] = xbc_ref[...]
    col_tile = 512
    for ct in range(conv_dim // col_tile):
        cols = slice(ct * col_tile, (ct + 1) * col_tile)
        acc = jnp.broadcast_to(convb_ref[:, cols], (lc, col_tile))
        for j in range(SSD_CONV):
            r0 = halo - (SSD_CONV - 1) + j
            acc = acc + convw_ref[j:j + 1, cols] * ext_ref[r0:r0 + lc, cols]
        act_ref[:, cols] = _silu(acc)
    ext_ref[0:halo, :] = xbc_ref[lc - halo:lc, :]

    x0 = dt_ref[...] + dtb_ref[...]
    dt = jnp.maximum(x0, 0.0) + jnp.log1p(jnp.exp(-jnp.abs(x0)))
    d_a = dt * (-jnp.exp(alog_ref[...]))
    tri = (lax.broadcasted_iota(I32, (lc, lc), 0) >= lax.broadcasted_iota(I32, (lc, lc), 1)).astype(F32)
    cs = jnp.dot(tri, d_a, precision=HIGHEST, preferred_element_type=F32)
    cs_last = cs[lc - 1:lc, :]
    cs_ref[...] = cs
    ecs_ref[...] = jnp.exp(cs)
    cdec_ref[...] = jnp.exp(cs_last)
    eye = (lax.broadcasted_iota(I32, (LANES, LANES), 0) == lax.broadcasted_iota(I32, (LANES, LANES), 1)).astype(F32)
    cst_ref[...] = _nt_dot(eye, cs, precision=HIGHEST)
    wt_ref[...] = _nt_dot(eye, jnp.exp(cs_last - cs) * dt, precision=HIGHEST)
    dtt_ref[...] = _nt_dot(eye, dt, precision=HIGHEST)

    causal = lax.broadcasted_iota(I32, (lc, lc), 0) >= lax.broadcasted_iota(I32, (lc, lc), 1)
    lane_head = lax.broadcasted_iota(I32, (lc, gw), 1) // SSD_HEAD_DIM
    lane_head_s = lax.broadcasted_iota(I32, (SSD_STATE, gw), 1) // SSD_HEAD_DIM
    n_xs = width
    n_b = SSD_GROUPS * SSD_STATE

    for g in range(SSD_GROUPS):
        gc = slice(g * gw, (g + 1) * gw)
        xs = act_ref[:, gc]
        b_g = act_ref[:, n_xs + g * SSD_STATE:n_xs + (g + 1) * SSD_STATE]
        c_g = act_ref[:, n_xs + n_b + g * SSD_STATE:n_xs + n_b + (g + 1) * SSD_STATE]
        xs_b = xs.astype(BF16)
        c_b = c_g.astype(BF16)
        cb = _nt_dot(c_b, b_g.astype(BF16))
        b_t = jnp.transpose(b_g)
        s_prev = state_ref[g]
        y_off = jnp.dot(c_b, s_prev.astype(BF16), preferred_element_type=F32)
        y = jnp.zeros((lc, gw), F32)
        s_new = jnp.zeros((SSD_STATE, gw), F32)
        for r in range(heads_per_group):
            h = g * heads_per_group + r
            diff = cs_ref[:, h:h + 1] - cst_ref[h:h + 1, :]
            seg = jnp.exp(jnp.where(causal, diff, -jnp.inf))
            m_h = (cb * seg * dtt_ref[h:h + 1, :]).astype(BF16)
            y_h = jnp.dot(m_h, xs_b, preferred_element_type=F32)
            bw = (b_t * wt_ref[h:h + 1, :]).astype(BF16)
            s_h = jnp.dot(bw, xs_b, preferred_element_type=F32)
            y = jnp.where(lane_head == r, y_h + ecs_ref[:, h:h + 1] * y_off, y)
            s_new = jnp.where(lane_head_s == r, cdec_ref[:, h:h + 1] * s_prev + s_h, s_new)
        state_ref[g] = s_new
        y = y + dskip_ref[:, gc] * xs
        y = y * _silu(z_ref[:, gc])
        y = y * lax.rsqrt(jnp.mean(y * y, axis=-1, keepdims=True) + SSD_NORM_EPS)
        o_ref[:, gc] = (y * normw_ref[:, gc]).astype(o_ref.dtype)


def _ssd(proj, dt_raw, conv_w, conv_b, dt_bias_pad, a_log_pad, d_skip_ch, norm_w, batch, seq, z_col, xbc_col, width):
    t = batch * seq
    nc = seq // SSD_CHUNK
    conv_dim = conv_w.shape[1]
    gw = width // SSD_GROUPS
    row = lambda b, c: b * nc + c
    return pl.pallas_call(
        _ssd_kernel,
        grid=(batch, nc),
        in_specs=[pl.BlockSpec((SSD_CHUNK, width), lambda b, c: (row(b, c), z_col)),
                  pl.BlockSpec((SSD_CHUNK, conv_dim), lambda b, c: (row(b, c), xbc_col)),
                  pl.BlockSpec((SSD_CHUNK, LANES), lambda b, c: (row(b, c), 0)),
                  pl.BlockSpec((SSD_CONV, conv_dim), lambda b, c: (0, 0)),
                  pl.BlockSpec((1, conv_dim), lambda b, c: (0, 0)),
                  pl.BlockSpec((1, LANES), lambda b, c: (0, 0)),
                  pl.BlockSpec((1, LANES), lambda b, c: (0, 0)),
                  pl.BlockSpec((1, width), lambda b, c: (0, 0)),
                  pl.BlockSpec((1, width), lambda b, c: (0, 0))],
        out_specs=pl.BlockSpec((SSD_CHUNK, width), lambda b, c: (row(b, c), 0)),
        out_shape=jax.ShapeDtypeStruct((t, width), BF16),
        scratch_shapes=[
            pltpu.VMEM((SUBLANES + SSD_CHUNK, conv_dim), F32),
            pltpu.VMEM((SSD_CHUNK, conv_dim), F32),
            pltpu.VMEM((SSD_GROUPS, SSD_STATE, gw), F32),
            pltpu.VMEM((SSD_CHUNK, LANES), F32),
            pltpu.VMEM((SSD_CHUNK, LANES), F32),
            pltpu.VMEM((LANES, SSD_CHUNK), F32),
            pltpu.VMEM((LANES, SSD_CHUNK), F32),
            pltpu.VMEM((LANES, SSD_CHUNK), F32),
            pltpu.VMEM((1, LANES), F32),
        ],
        compiler_params=pltpu.CompilerParams(
            dimension_semantics=("parallel", "arbitrary"), vmem_limit_bytes=VMEM_LIMIT),
        name="ssd_mixer",
    )(proj, proj, dt_raw, conv_w, conv_b.reshape(1, conv_dim), dt_bias_pad, a_log_pad,
      d_skip_ch.reshape(1, width), norm_w.reshape(1, width))


def _router_kernel(x_ref, nw_ref, wr_ref, br_ref, h_ref, meta_ref, cnt_ref, carry_ref):
    tm = x_ref.shape[0]

    @pl.when(pl.program_id(0) == 0)
    def _():
        carry_ref[...] = jnp.zeros_like(carry_ref)

    x = x_ref[...]
    h = x * lax.rsqrt(jnp.mean(x * x, axis=-1, keepdims=True) + NORM_EPS) * nw_ref[...]
    h_ref[...] = h

    wp = wr_ref[...]
    h_hi = h.astype(BF16)
    rem = h - h_hi.astype(F32)
    h_mid = rem.astype(BF16)
    h_lo = (rem - h_mid.astype(F32)).astype(BF16)
    parts = (jnp.dot(h_hi, wp, preferred_element_type=F32) + jnp.dot(h_mid, wp, preferred_element_type=F32)
             + jnp.dot(h_lo, wp, preferred_element_type=F32))
    logits = parts
    for k in range(1, 3):
        logits = logits + pltpu.roll(parts, LANES - k * ROUTER_PART_LANES, 1)
    logits = logits + br_ref[...]
    lane = lax.broadcasted_iota(I32, (tm, LANES), 1)
    neg_inf = jnp.float32(-jnp.inf)

    lane_f = lane.astype(F32)

    def first_argmax(v, vmax):
        return jnp.min(jnp.where(v == vmax, lane_f, float(LANES)), axis=1, keepdims=True).astype(I32)

    lg = jnp.where(lane < N_EXPERT_GROUPS, logits, neg_inf)
    g_max = jnp.max(lg, axis=1, keepdims=True)
    g_idx = first_argmax(lg, g_max)
    g_p = 1.0 / jnp.sum(jnp.exp(lg - g_max), axis=1, keepdims=True)

    e_lo = ROUTER_EXPERT_LANE0 + EXPERTS_PER_GROUP * g_idx
    le = jnp.where((lane >= e_lo) & (lane < e_lo + EXPERTS_PER_GROUP), logits, neg_inf)
    e_max = jnp.max(le, axis=1, keepdims=True)
    i1 = first_argmax(le, e_max)
    e_sum = jnp.sum(jnp.exp(le - e_max), axis=1, keepdims=True)
    le2 = jnp.where(lane == i1, neg_inf, le)
    e_max2 = jnp.max(le2, axis=1, keepdims=True)
    i2 = first_argmax(le2, e_max2)
    p1 = 1.0 / e_sum
    p2 = jnp.exp(e_max2 - e_max) / e_sum
    gate1 = g_p * (p1 / (p1 + p2))
    gate2 = g_p * (p2 / (p1 + p2))

    hot = ((lane == i1) | (lane == i2))
    hot_b = hot.astype(BF16)
    strict = (lax.broadcasted_iota(I32, (tm, tm), 0) > lax.broadcasted_iota(I32, (tm, tm), 1)).astype(BF16)
    before = jnp.dot(strict, hot_b, preferred_element_type=F32) + carry_ref[0:1, :]
    rank1 = jnp.sum(jnp.where(lane == i1, before, 0.0), axis=1, keepdims=True)
    rank2 = jnp.sum(jnp.where(lane == i2, before, 0.0), axis=1, keepdims=True)
    carry_ref[0:1, :] = carry_ref[0:1, :] + jnp.sum(hot.astype(F32), axis=0, keepdims=True)
    cnt_ref[...] = jnp.broadcast_to(carry_ref[0:1, :], cnt_ref.shape)

    vals = [(i1 - ROUTER_EXPERT_LANE0).astype(F32), (i2 - ROUTER_EXPERT_LANE0).astype(F32), rank1, rank2, gate1, gate2]
    meta = jnp.zeros((tm, LANES), F32)
    for c, v in enumerate(vals):
        meta = jnp.where(lane == c, v, meta)
    meta_ref[...] = meta


def _router(x1, norm_w, w_router, b_router):
    t, d = x1.shape
    return pl.pallas_call(
        _router_kernel,
        grid=(t // ROUTER_TILE,),
        in_specs=[pl.BlockSpec((ROUTER_TILE, d), lambda i: (i, 0)),
                  pl.BlockSpec((1, d), lambda i: (0, 0)),
                  pl.BlockSpec((d, LANES), lambda i: (0, 0)),
                  pl.BlockSpec((1, LANES), lambda i: (0, 0))],
        out_specs=[pl.BlockSpec((ROUTER_TILE, d), lambda i: (i, 0)),
                   pl.BlockSpec((ROUTER_TILE, LANES), lambda i: (i, 0)),
                   pl.BlockSpec((SUBLANES, LANES), lambda i: (0, 0))],
        out_shape=[jax.ShapeDtypeStruct((t, d), F32),
                   jax.ShapeDtypeStruct((t, LANES), F32),
                   jax.ShapeDtypeStruct((SUBLANES, LANES), F32)],
        scratch_shapes=[pltpu.VMEM((SUBLANES, LANES), F32)],
        compiler_params=pltpu.CompilerParams(dimension_semantics=("arbitrary",), vmem_limit_bytes=VMEM_LIMIT),
        name="router",
    )(x1, norm_w.reshape(1, d), w_router, b_router)


def _wait_rows(src_ref, dst_ref, n, sem):
    done = 0
    for rows in (64, SUBLANES, 1):
        trips = lax.shift_right_logical(n - done, rows.bit_length() - 1)

        def body(r, c, rows=rows):
            pltpu.make_async_copy(src_ref.at[pl.ds(0, rows)], dst_ref.at[pl.ds(0, rows)], sem).wait()
            return c
        lax.fori_loop(0, trips, body, 0)
        done = done + trips * rows


def _for_rows(n, body):
    n_groups = lax.shift_right_logical(n, ROW_UNROLL.bit_length() - 1)

    def group(g, c):
        base = pl.multiple_of(g * ROW_UNROLL, ROW_UNROLL)
        for u in range(ROW_UNROLL):
            body(base + u, lambda buf, u=u: buf.at[pl.ds(base, ROW_UNROLL)].at[pl.ds(u, 1)])
        return c
    lax.fori_loop(0, n_groups, group, 0)

    def tail(r, c):
        body(r, lambda buf: buf.at[pl.ds(r, 1)])
        return c
    lax.fori_loop(n_groups * ROW_UNROLL, n, tail, 0)


def _for_row_blocks(n_rows, fn):
    n_blk = lax.shift_right_logical(n_rows + (MOE_SUB - 1), MOE_SUB.bit_length() - 1)
    for blocks in range(1, MOE_ROWS // MOE_SUB + 1):
        pl.when(n_blk == blocks)(functools.partial(fn, blocks * MOE_SUB))


def _moe_up_kernel(n_items_ref, item_e_ref, item_start_ref, item_n_ref, flat_ref,
                   h_hbm, wg_ref, wu_ref, o_ref, xbuf_ref, xb_ref, sem, *, n_tokens):
    i = pl.program_id(0)
    j = pl.program_id(1)
    n_items = n_items_ref[0]
    valid = i < n_items
    conv_rows = 64

    def start_gather(item):
        start = item_start_ref[item]

        def body(r, row_of):
            tok = flat_ref[start + r] & (n_tokens - 1)
            pltpu.make_async_copy(h_hbm.at[pl.ds(tok, 1)], row_of(xbuf_ref), sem).start()
        _for_rows(item_n_ref[item], body)

    @pl.when((i == 0) & (j == 0))
    def _():
        xbuf_ref[...] = jnp.zeros_like(xbuf_ref)
        start_gather(0)

    @pl.when(valid & (j == 0))
    def _():
        _wait_rows(h_hbm, xbuf_ref, item_n_ref[i], sem)

        def convert(s, c):
            rows = pl.ds(pl.multiple_of(s * conv_rows, conv_rows), conv_rows)
            xb_ref[rows, :] = xbuf_ref[rows, :].astype(BF16)
            return c
        lax.fori_loop(0, MOE_ROWS // conv_rows, convert, 0)

        @pl.when(i + 1 < n_items)
        def _():
            start_gather(i + 1)

    o_ref[...] = jnp.zeros_like(o_ref)

    def compute(m):
        xs = xb_ref[0:m, :]
        g = jnp.dot(xs, wg_ref[...].astype(BF16), preferred_element_type=F32)
        u = jnp.dot(xs, wu_ref[...].astype(BF16), preferred_element_type=F32)
        o_ref[0:m, :] = (_silu(g) * u).astype(o_ref.dtype)

    @pl.when(valid)
    def _():
        _for_row_blocks(item_n_ref[i], compute)


def _moe_up(sched, h2, w_gate, w_up, n_items_max):
    n_tokens, _ = h2.shape
    _, d, ff = w_gate.shape
    nj = ff // MOE_FF_TILE

    def w_map(i, j, n_items, item_e, *_):
        ii = jnp.clip(i, 0, jnp.maximum(n_items[0] - 1, 0))
        return (item_e[ii], 0, jnp.where(i < n_items[0], j, nj - 1))

    def o_map(i, j, *_):
        return (i, j)

    return pl.pallas_call(
        functools.partial(_moe_up_kernel, n_tokens=n_tokens),
        grid_spec=pltpu.PrefetchScalarGridSpec(
            num_scalar_prefetch=5,
            grid=(n_items_max, nj),
            in_specs=[pl.BlockSpec(memory_space=pl.ANY),
                      pl.BlockSpec((None, d, MOE_FF_TILE), w_map),
                      pl.BlockSpec((None, d, MOE_FF_TILE), w_map)],
            out_specs=pl.BlockSpec((MOE_ROWS, MOE_FF_TILE), o_map),
            scratch_shapes=[pltpu.VMEM((MOE_ROWS, d), F32),
                            pltpu.VMEM((MOE_ROWS, d), BF16),
                            pltpu.SemaphoreType.DMA(())]),
        out_shape=jax.ShapeDtypeStruct((n_items_max * MOE_ROWS, ff), BF16),
        compiler_params=pltpu.CompilerParams(
            dimension_semantics=("arbitrary", "arbitrary"), vmem_limit_bytes=VMEM_LIMIT),
        name="moe_up",
    )(*sched, h2, w_gate, w_up)


def _moe_down_kernel(n_items_ref, item_e_ref, item_start_ref, item_n_ref, flat_ref,
                     h_ref, wd_ref, y_hbm, *scratch):
    i = pl.program_id(0)
    j = pl.program_id(1)
    valid = i < n_items_ref[0]
    tn = wd_ref.shape[1]
    ybufs = scratch[:MOE_DOWN_SPLIT]
    sems = scratch[MOE_DOWN_SPLIT:2 * MOE_DOWN_SPLIT]
    pending_ref = scratch[2 * MOE_DOWN_SPLIT]

    @pl.when((i == 0) & (j == 0))
    def _():
        for jj in range(MOE_DOWN_SPLIT):
            pending_ref[jj] = 0

    def drain(jj):
        y_part = y_hbm.at[:, pl.ds(jj * tn, tn)]
        _wait_rows(ybufs[jj], y_part, pending_ref[jj], sems[jj])
        pending_ref[jj] = 0

    for jj in range(MOE_DOWN_SPLIT):
        @pl.when(valid & (j == jj))
        def _(jj=jj):
            ybuf = ybufs[jj]
            y_part = y_hbm.at[:, pl.ds(jj * tn, tn)]
            drain(jj)
            n = item_n_ref[i]
            start = item_start_ref[i]

            def compute(m):
                ybuf[0:m, :] = jnp.dot(h_ref[0:m, :], wd_ref[...].astype(BF16), preferred_element_type=F32)
            _for_row_blocks(n, compute)

            def issue(r, row_of):
                dest = flat_ref[start + r]
                pltpu.make_async_copy(row_of(ybuf), y_part.at[pl.ds(dest, 1)], sems[jj]).start()
            _for_rows(n, issue)
            pending_ref[jj] = n

    @pl.when((i == pl.num_programs(0) - 1) & (j == MOE_DOWN_SPLIT - 1))
    def _():
        for jj in range(MOE_DOWN_SPLIT):
            drain(jj)


def _moe_down(sched, h_items, w_down, n_items_max, n_tokens):
    _, ff, d = w_down.shape
    tn = d // MOE_DOWN_SPLIT

    def w_map(i, j, n_items, item_e, *_):
        ii = jnp.clip(i, 0, jnp.maximum(n_items[0] - 1, 0))
        return (item_e[ii], 0, jnp.where(i < n_items[0], j, MOE_DOWN_SPLIT - 1))

    def h_map(i, j, n_items, *_):
        return (jnp.clip(i, 0, jnp.maximum(n_items[0] - 1, 0)), 0)

    return pl.pallas_call(
        _moe_down_kernel,
        grid_spec=pltpu.PrefetchScalarGridSpec(
            num_scalar_prefetch=5,
            grid=(n_items_max, MOE_DOWN_SPLIT),
            in_specs=[pl.BlockSpec((MOE_ROWS, ff), h_map),
                      pl.BlockSpec((None, ff, tn), w_map)],
            out_specs=pl.BlockSpec(memory_space=pl.ANY),
            scratch_shapes=([pltpu.VMEM((MOE_ROWS, tn), F32)] * MOE_DOWN_SPLIT
                            + [pltpu.SemaphoreType.DMA(())] * MOE_DOWN_SPLIT
                            + [pltpu.SMEM((MOE_DOWN_SPLIT,), I32)])),
        out_shape=jax.ShapeDtypeStruct((TOP_K * n_tokens, d), F32),
        compiler_params=pltpu.CompilerParams(
            dimension_semantics=("arbitrary", "arbitrary"), vmem_limit_bytes=VMEM_LIMIT),
        name="moe_down",
    )(*sched, h_items, w_down)


def _combine_kernel(x_ref, meta_ref, y0_ref, y1_ref, o_ref):
    o_ref[...] = x_ref[...] + meta_ref[:, 4:5] * y0_ref[...] + meta_ref[:, 5:6] * y1_ref[...]


def _combine(x1, meta, y_slots):
    t, d = x1.shape
    tm = ROUTER_TILE
    nt = t // tm
    return pl.pallas_call(
        _combine_kernel,
        grid=(nt,),
        in_specs=[pl.BlockSpec((tm, d), lambda i: (i, 0)),
                  pl.BlockSpec((tm, LANES), lambda i: (i, 0)),
                  pl.BlockSpec((tm, d), lambda i: (i, 0)),
                  pl.BlockSpec((tm, d), lambda i: (nt + i, 0))],
        out_specs=pl.BlockSpec((tm, d), lambda i: (i, 0)),
        out_shape=jax.ShapeDtypeStruct((t, d), F32),
        compiler_params=pltpu.CompilerParams(dimension_semantics=("parallel",), vmem_limit_bytes=VMEM_LIMIT),
        name="moe_combine",
    )(x1, meta, y_slots, y_slots)


def _moe_schedule(meta, counts_f, n_tokens, n_items_max):
    ids = jnp.transpose(meta[:, 0:2 * TOP_K]).astype(I32)
    e_id = ids[0:TOP_K]
    rank = ids[TOP_K:2 * TOP_K]
    counts = counts_f[0, ROUTER_EXPERT_LANE0:ROUTER_EXPERT_LANE0 + N_EXPERTS].astype(I32)
    starts = jnp.cumsum(counts) - counts
    pos = starts[e_id] + rank
    flat = jnp.arange(TOP_K * n_tokens, dtype=I32)
    flat_sorted = jnp.zeros((TOP_K * n_tokens,), I32).at[pos.reshape(-1)].set(flat, unique_indices=True)
    chunks = (counts + MOE_ROWS - 1) // MOE_ROWS
    chunk_end = jnp.cumsum(chunks)
    n_items = chunk_end[-1]
    item = jnp.arange(n_items_max, dtype=I32)
    item_e = jnp.minimum(jnp.sum(item[:, None] >= chunk_end[None, :], axis=1), N_EXPERTS - 1).astype(I32)
    local = item - (chunk_end - chunks)[item_e]
    item_start = starts[item_e] + local * MOE_ROWS
    item_n = jnp.clip(counts[item_e] - local * MOE_ROWS, 0, MOE_ROWS)
    in_range = item < n_items
    item_start = jnp.where(in_range, item_start, 0).astype(I32)
    item_n = jnp.where(in_range, item_n, 0).astype(I32)
    return (n_items.reshape(1).astype(I32), item_e, item_start, item_n, flat_sorted)


def _layer(x, positions, norm_attn_w, w_in, q_norm_w, k_norm_w, conv_w, conv_b, dt_bias, a_log, d_skip,
           ssd_norm_w, w_out, norm_ffn_w, router_group_w, router_group_b, router_expert_w, router_expert_b,
           w_gate, w_up, w_down):
    batch, seq, d = x.shape
    t = batch * seq
    attn_width = d // 2
    n_heads = attn_width // HEAD_DIM
    ssd_width = d - attn_width
    ssd_heads = ssd_width // SSD_HEAD_DIM
    conv_dim = ssd_width + 2 * SSD_GROUPS * SSD_STATE
    main_cols = 3 * attn_width + ssd_width + conv_dim
    assert w_in.shape[1] == main_cols + ssd_heads and ssd_heads <= LANES
    assert seq % (SSD_CHUNK) == 0 and t % IN_TM == 0 and (t & (t - 1)) == 0

    x2d = x.reshape(t, d)
    h = _rmsnorm_cast(x2d, norm_attn_w)
    wt_in = jnp.transpose(w_in).astype(BF16)
    proj = _matmul_nt(h, wt_in, main_cols, IN_TM, IN_TN, "in_proj")
    wt_dt = jnp.pad(wt_in[main_cols:, :], ((0, LANES - ssd_heads), (0, 0)))
    dt_raw = _matmul_nt(h, wt_dt, LANES, IN_TM, LANES, "dt_proj")

    half = HEAD_DIM // 2
    inv_freq = jnp.power(jnp.float32(ROPE_THETA), -jnp.arange(half, dtype=F32) / half)
    rope_tab = jnp.stack([jnp.concatenate([inv_freq, inv_freq]),
                          jnp.concatenate([-jnp.ones((half,), F32), jnp.ones((half,), F32)])])
    attn = _attention(proj, positions.reshape(t, 1), rope_tab, q_norm_w, k_norm_w, batch, seq, n_heads)

    pad_heads = lambda v: jnp.pad(v.astype(F32), (0, LANES - ssd_heads)).reshape(1, LANES)
    ssd = _ssd(proj, dt_raw, conv_w, conv_b, pad_heads(dt_bias), pad_heads(a_log),
               jnp.repeat(d_skip.astype(F32), SSD_HEAD_DIM), ssd_norm_w, batch, seq,
               z_col=(3 * attn_width) // ssd_width, xbc_col=(3 * attn_width + ssd_width) // conv_dim,
               width=ssd_width)

    x1 = _out_proj(attn, ssd, w_out.astype(BF16), x2d, OUT_TM, OUT_TN)

    n_router = N_EXPERT_GROUPS + N_EXPERTS
    assert n_router <= ROUTER_PART_LANES and 3 * ROUTER_PART_LANES <= LANES
    w_cat = jnp.concatenate([router_group_w, router_expert_w], axis=1).astype(F32)
    w_parts, rem = [], w_cat
    for _ in range(3):
        part = rem.astype(BF16)
        w_parts.append(jnp.pad(part, ((0, 0), (0, ROUTER_PART_LANES - n_router))))
        rem = rem - part.astype(F32)
    w_router = jnp.pad(jnp.concatenate(w_parts, axis=1), ((0, 0), (0, LANES - 3 * ROUTER_PART_LANES)))
    b_router = jnp.pad(jnp.concatenate([router_group_b, router_expert_b]), (0, LANES - n_router)).reshape(1, LANES)
    h2, meta, counts_f = _router(x1, norm_ffn_w, w_router, b_router)

    n_items_max = N_EXPERTS + (TOP_K * t) // MOE_ROWS
    sched = _moe_schedule(meta, counts_f, t, n_items_max)
    h_items = _moe_up(sched, h2, w_gate, w_up, n_items_max)
    y_slots = _moe_down(sched, h_items, w_down, n_items_max, t)
    out = _combine(x1, meta, y_slots)
    return out.reshape(batch, seq, d)


def kernel(x, positions, norm_attn_w, w_in, q_norm_w, k_norm_w, conv_w, conv_b, dt_bias, a_log, d_skip, ssd_norm_w, w_out, norm_ffn_w, router_group_w, router_group_b, router_expert_w, router_expert_b, w_gate, w_up, w_down):
    for layer in range(norm_attn_w.shape[0]):
        x = _layer(x, positions, norm_attn_w[layer], w_in[layer], q_norm_w[layer], k_norm_w[layer],
                   conv_w[layer], conv_b[layer], dt_bias[layer], a_log[layer], d_skip[layer],
                   ssd_norm_w[layer], w_out[layer], norm_ffn_w[layer], router_group_w[layer],
                   router_group_b[layer], router_expert_w[layer], router_expert_b[layer],
                   w_gate[layer], w_up[layer], w_down[layer])
    return x
```

```python
import functools

import jax
import jax.numpy as jnp
from jax import lax
from jax.experimental import pallas as pl
from jax.experimental.pallas import tpu as pltpu

F32 = jnp.float32
BF16 = jnp.bfloat16
I32 = jnp.int32
HIGHEST = lax.Precision.HIGHEST

HEAD_DIM = 128
DILATED_BRANCHES = ((128, 1), (512, 4), (2048, 16))
ATTN_BLOCK = 128
ROPE_THETA = 10000.0
SSD_HEAD_DIM = 64
SSD_GROUPS = 8
SSD_STATE = 128
SSD_CONV = 4
SSD_CHUNK = 256
N_EXPERT_GROUPS = 4
EXPERTS_PER_GROUP = 8
N_EXPERTS = N_EXPERT_GROUPS * EXPERTS_PER_GROUP
TOP_K = 2
NORM_EPS = 1e-6
SSD_NORM_EPS = 1e-5

LANES = 128
SUBLANES = 8
VMEM_LIMIT = 56 * 1024 * 1024

ROW_TILE = 512
IN_TM, IN_TN = 1024, 512
OUT_TM, OUT_TN = 1024, 512
MOE_ROWS = 1024
MOE_SUB = 128
ROW_UNROLL = 8
MOE_FF_TILE = 256
MOE_DOWN_SPLIT = 2
ROUTER_TILE = 256
ROUTER_EXPERT_LANE0 = N_EXPERT_GROUPS
ROUTER_PART_LANES = 40


def _silu(v):
    return v * (1.0 / (1.0 + jnp.exp(-v)))


def _nt_dot(a, b, **kw):
    return lax.dot_general(a, b, (((1,), (1,)), ((), ())), preferred_element_type=F32, **kw)


def _rmsnorm_cast_kernel(x_ref, w_ref, o_ref):
    x = x_ref[...]
    y = x * lax.rsqrt(jnp.mean(x * x, axis=-1, keepdims=True) + NORM_EPS)
    o_ref[...] = (y * w_ref[...]).astype(o_ref.dtype)


def _rmsnorm_cast(x2d, w):
    t, d = x2d.shape
    return pl.pallas_call(
        _rmsnorm_cast_kernel,
        grid=(t // ROW_TILE,),
        in_specs=[pl.BlockSpec((ROW_TILE, d), lambda i: (i, 0)),
                  pl.BlockSpec((1, d), lambda i: (0, 0))],
        out_specs=pl.BlockSpec((ROW_TILE, d), lambda i: (i, 0)),
        out_shape=jax.ShapeDtypeStruct((t, d), BF16),
        compiler_params=pltpu.CompilerParams(dimension_semantics=("parallel",), vmem_limit_bytes=VMEM_LIMIT),
        name="rmsnorm_cast",
    )(x2d, w.reshape(1, d))


def _matmul_nt_kernel(a_ref, wt_ref, o_ref):
    o_ref[...] = _nt_dot(a_ref[...], wt_ref[...].astype(BF16)).astype(o_ref.dtype)


def _matmul_nt(a, wt, n_rows, tm, tn, name):
    m, kdim = a.shape
    return pl.pallas_call(
        _matmul_nt_kernel,
        grid=(m // tm, n_rows // tn),
        in_specs=[pl.BlockSpec((tm, kdim), lambda i, j: (i, 0)),
                  pl.BlockSpec((tn, kdim), lambda i, j: (j, 0))],
        out_specs=pl.BlockSpec((tm, tn), lambda i, j: (i, j)),
        out_shape=jax.ShapeDtypeStruct((m, n_rows), F32),
        compiler_params=pltpu.CompilerParams(
            dimension_semantics=("parallel", "parallel"), vmem_limit_bytes=VMEM_LIMIT),
        name=name,
    )(a, wt)


def _out_proj_kernel(a1_ref, a2_ref, w_ref, x_ref, o_ref):
    k1 = a1_ref.shape[1]
    acc = jnp.dot(a1_ref[...], w_ref[0:k1, :].astype(BF16), preferred_element_type=F32)
    acc = acc + jnp.dot(a2_ref[...], w_ref[k1:, :].astype(BF16), preferred_element_type=F32)
    o_ref[...] = x_ref[...] + acc


def _out_proj(a1, a2, w, x2d, tm, tn):
    m, k1 = a1.shape
    k2 = a2.shape[1]
    kdim, n = w.shape
    return pl.pallas_call(
        _out_proj_kernel,
        grid=(m // tm, n // tn),
        in_specs=[pl.BlockSpec((tm, k1), lambda i, j: (i, 0)),
                  pl.BlockSpec((tm, k2), lambda i, j: (i, 0)),
                  pl.BlockSpec((kdim, tn), lambda i, j: (0, j)),
                  pl.BlockSpec((tm, tn), lambda i, j: (i, j))],
        out_specs=pl.BlockSpec((tm, tn), lambda i, j: (i, j)),
        out_shape=jax.ShapeDtypeStruct((m, n), F32),
        compiler_params=pltpu.CompilerParams(
            dimension_semantics=("parallel", "parallel"), vmem_limit_bytes=VMEM_LIMIT),
        name="out_proj",
    )(a1, a2, w, x2d)


def _attn_kernel(pos_ref, rope_ref, qw_ref, kw_ref, q_ref, k_ref, v_ref, o_ref,
                 cos_ref, sin_ref, qf_ref, kf_ref, qd_ref, kd_ref, vd_ref,
                 bias_ref, s_ref, p_ref, m_ref, o_br_ref, lse_br_ref, od_ref, lsed_ref):
    seq = q_ref.shape[0]
    blk = ATTN_BLOCK
    n_blocks = seq // blk

    @pl.when(pl.program_id(1) == 0)
    def _():
        ang = pos_ref[...].astype(F32) * rope_ref[0:1, :]
        cos_ref[...] = jnp.cos(ang)
        sin_ref[...] = jnp.sin(ang) * rope_ref[1:2, :]

    def norm_rot(t_ref, w_ref, dst_ref, scale):
        def body(i, c):
            rows = pl.ds(pl.multiple_of(i * 256, 256), 256)
            t = t_ref[rows, :]
            y = t * lax.rsqrt(jnp.mean(t * t, axis=-1, keepdims=True) + NORM_EPS) * w_ref[...]
            y = y * cos_ref[rows, :] + pltpu.roll(y, HEAD_DIM // 2, 1) * sin_ref[rows, :]
            dst_ref[rows, :] = y * scale if scale != 1.0 else y
            return c
        lax.fori_loop(0, seq // 256, body, 0, unroll=2)

    norm_rot(q_ref, qw_ref, qf_ref, HEAD_DIM ** -0.5)
    norm_rot(k_ref, kw_ref, kf_ref, 1.0)

    kd_ref[0:blk, :] = jnp.zeros((blk, HEAD_DIM), BF16)
    vd_ref[0:blk, 0:HEAD_DIM] = jnp.zeros((blk, HEAD_DIM), BF16)
    vd_ref[:, HEAD_DIM:] = jnp.ones((blk + seq, HEAD_DIM), BF16)

    for bi, (window, dil) in enumerate(DILATED_BRANCHES):
        n_back = window // dil
        sub_len = seq // dil
        nb = sub_len // blk
        natural = dil == 1
        kw = blk if nb == 1 else 2 * blk
        k_off = blk if nb == 1 else 0
        for r in range(dil):
            src = slice(None) if natural else pl.ds(r, sub_len, stride=dil)
            dst = slice(r * sub_len, (r + 1) * sub_len)
            dstp = slice(blk + r * sub_len, blk + (r + 1) * sub_len)
            qd_ref[dst, :] = qf_ref[src, :].astype(BF16)
            kd_ref[dstp, :] = kf_ref[src, :].astype(BF16)
            vd_ref[dstp, 0:HEAD_DIM] = v_ref[src, :].astype(BF16)

        o_out = o_br_ref.at[bi] if natural else od_ref
        lse_out = lse_br_ref.at[bi] if natural else lsed_ref

        qi = lax.broadcasted_iota(I32, (blk, kw), 0)
        kj = lax.broadcasted_iota(I32, (blk, kw), 1)
        dist = (blk - k_off) + qi - kj
        ok = (dist >= 0) & (dist <= n_back)
        bias_ref[0, :, 0:kw] = jnp.where(ok, 0.0, -jnp.inf)
        bias_ref[1, :, 0:kw] = jnp.where(ok & (kj >= blk - k_off), 0.0, -jnp.inf)

        def scores(j, c):
            r0 = pl.multiple_of(j * blk, blk)
            s_ref[pl.ds(r0, blk), 0:kw] = _nt_dot(qd_ref[pl.ds(r0, blk), :], kd_ref[pl.ds(r0 + k_off, kw), :])
            return c
        lax.fori_loop(0, n_blocks, scores, 0, unroll=8)

        def softmax(j, c):
            r0 = pl.multiple_of(j * blk, blk)
            first = jnp.where((j & (nb - 1)) == 0, 1, 0)
            s = s_ref[pl.ds(r0, blk), 0:kw] + bias_ref[first, :, 0:kw]
            m = jnp.max(s, axis=1, keepdims=True)
            p_ref[pl.ds(r0, blk), 0:kw] = jnp.exp(s - m).astype(BF16)
            m_ref[pl.ds(r0, blk), :] = jnp.broadcast_to(m, (blk, HEAD_DIM))
            return c
        lax.fori_loop(0, n_blocks, softmax, 0, unroll=4)

        def weighted(j, c):
            r0 = pl.multiple_of(j * blk, blk)
            acc = jnp.dot(p_ref[pl.ds(r0, blk), 0:kw], vd_ref[pl.ds(r0 + k_off, kw), :], preferred_element_type=F32)
            l = acc[:, HEAD_DIM:]
            o_out[pl.ds(r0, blk), :] = acc[:, 0:HEAD_DIM] / l
            lse_out[pl.ds(r0, blk), :] = m_ref[pl.ds(r0, blk), :] + jnp.log(l)
            return c
        lax.fori_loop(0, n_blocks, weighted, 0, unroll=8)

        if not natural:
            for r in range(dil):
                src = slice(r * sub_len, (r + 1) * sub_len)
                dst = pl.ds(r, sub_len, stride=dil)
                o_br_ref[bi, dst, :] = od_ref[src, :]
                lse_br_ref[bi, dst, :] = lsed_ref[src, :]

    n_br = len(DILATED_BRANCHES)

    def merge(i, c):
        rows = pl.ds(pl.multiple_of(i * 256, 256), 256)
        lses = [lse_br_ref[b, rows, :] for b in range(n_br)]
        top = functools.reduce(jnp.maximum, lses)
        num = jnp.zeros((256, HEAD_DIM), F32)
        den = jnp.zeros((256, HEAD_DIM), F32)
        for b in range(n_br):
            e = jnp.exp(lses[b] - top)
            num = num + e * o_br_ref[b, rows, :]
            den = den + e
        o_ref[rows, :] = (num / den).astype(o_ref.dtype)
        return c
    lax.fori_loop(0, seq // 256, merge, 0)


def _attention(proj, pos_col, rope_tab, q_norm_w, k_norm_w, batch, seq, n_heads):
    t = batch * seq
    nbr = len(DILATED_BRANCHES)
    qkv_spec = lambda off: pl.BlockSpec((seq, HEAD_DIM), lambda b, h: (b, off + h))
    return pl.pallas_call(
        _attn_kernel,
        grid=(batch, n_heads),
        in_specs=[pl.BlockSpec((seq, 1), lambda b, h: (b, 0)),
                  pl.BlockSpec((2, HEAD_DIM), lambda b, h: (0, 0)),
                  pl.BlockSpec((1, HEAD_DIM), lambda b, h: (0, 0)),
                  pl.BlockSpec((1, HEAD_DIM), lambda b, h: (0, 0)),
                  qkv_spec(0), qkv_spec(n_heads), qkv_spec(2 * n_heads)],
        out_specs=pl.BlockSpec((seq, HEAD_DIM), lambda b, h: (b, h)),
        out_shape=jax.ShapeDtypeStruct((t, n_heads * HEAD_DIM), BF16),
        scratch_shapes=[
            pltpu.VMEM((seq, HEAD_DIM), F32),
            pltpu.VMEM((seq, HEAD_DIM), F32),
            pltpu.VMEM((seq, HEAD_DIM), F32),
            pltpu.VMEM((seq, HEAD_DIM), F32),
            pltpu.VMEM((seq, HEAD_DIM), BF16),
            pltpu.VMEM((ATTN_BLOCK + seq, HEAD_DIM), BF16),
            pltpu.VMEM((ATTN_BLOCK + seq, 2 * HEAD_DIM), BF16),
            pltpu.VMEM((2, ATTN_BLOCK, 2 * ATTN_BLOCK), F32),
            pltpu.VMEM((seq, 2 * ATTN_BLOCK), F32),
            pltpu.VMEM((seq, 2 * ATTN_BLOCK), BF16),
            pltpu.VMEM((seq, HEAD_DIM), F32),
            pltpu.VMEM((nbr, seq, HEAD_DIM), F32),
            pltpu.VMEM((nbr, seq, HEAD_DIM), F32),
            pltpu.VMEM((seq, HEAD_DIM), F32),
            pltpu.VMEM((seq, HEAD_DIM), F32),
        ],
        compiler_params=pltpu.CompilerParams(
            dimension_semantics=("parallel", "arbitrary"), vmem_limit_bytes=VMEM_LIMIT),
        name="dilated_attention",
    )(pos_col, rope_tab, q_norm_w.reshape(1, HEAD_DIM), k_norm_w.reshape(1, HEAD_DIM), proj, proj, proj)


def _ssd_kernel(z_ref, xbc_ref, dt_ref, convw_ref, convb_ref, dtb_ref, alog_ref, dskip_ref, normw_ref,
                o_ref, ext_ref, act_ref, state_ref, cs_ref, ecs_ref, cst_ref, wt_ref, dtt_ref, cdec_ref):
    lc = SSD_CHUNK
    width = o_ref.shape[1]
    gw = width // SSD_GROUPS
    heads_per_group = gw // SSD_HEAD_DIM
    conv_dim = xbc_ref.shape[1]
    halo = SUBLANES

    @pl.when(pl.program_id(1) == 0)
    def _():
        ext_ref[0:halo, :] = jnp.zeros((halo, conv_dim), F32)
        state_ref[...] = jnp.zeros_like(state_ref)

    ext_ref[halo:halo + lc, :] = xbc_ref[...]
    col_tile = 512
    for ct in range(conv_dim // col_tile):
        cols = slice(ct * col_tile, (ct + 1) * col_tile)
        acc = jnp.broadcast_to(convb_ref[:, cols], (lc, col_tile))
        for j in range(SSD_CONV):
            r0 = halo - (SSD_CONV - 1) + j
            acc = acc + convw_ref[j:j + 1, cols] * ext_ref[r0:r0 + lc, cols]
        act_ref[:, cols] = _silu(acc)
    ext_ref[0:halo, :] = xbc_ref[lc - halo:lc, :]

    x0 = dt_ref[...] + dtb_ref[...]
    dt = jnp.maximum(x0, 0.0) + jnp.log1p(jnp.exp(-jnp.abs(x0)))
    d_a = dt * (-jnp.exp(alog_ref[...]))
    tri = (lax.broadcasted_iota(I32, (lc, lc), 0) >= lax.broadcasted_iota(I32, (lc, lc), 1)).astype(F32)
    cs = jnp.dot(tri, d_a, precision=HIGHEST, preferred_element_type=F32)
    cs_last = cs[lc - 1:lc, :]
    cs_ref[...] = cs
    ecs_ref[...] = jnp.exp(cs)
    cdec_ref[...] = jnp.exp(cs_last)
    eye = (lax.broadcasted_iota(I32, (LANES, LANES), 0) == lax.broadcasted_iota(I32, (LANES, LANES), 1)).astype(F32)
    cst_ref[...] = _nt_dot(eye, cs, precision=HIGHEST)
    wt_ref[...] = _nt_dot(eye, jnp.exp(cs_last - cs) * dt, precision=HIGHEST)
    dtt_ref[...] = _nt_dot(eye, dt, precision=HIGHEST)

    causal = lax.broadcasted_iota(I32, (lc, lc), 0) >= lax.broadcasted_iota(I32, (lc, lc), 1)
    lane_head = lax.broadcasted_iota(I32, (lc, gw), 1) // SSD_HEAD_DIM
    lane_head_s = lax.broadcasted_iota(I32, (SSD_STATE, gw), 1) // SSD_HEAD_DIM
    n_xs = width
    n_b = SSD_GROUPS * SSD_STATE

    for g in range(SSD_GROUPS):
        gc = slice(g * gw, (g + 1) * gw)
        xs = act_ref[:, gc]
        b_g = act_ref[:, n_xs + g * SSD_STATE:n_xs + (g + 1) * SSD_STATE]
        c_g = act_ref[:, n_xs + n_b + g * SSD_STATE:n_xs + n_b + (g + 1) * SSD_STATE]
        xs_b = xs.astype(BF16)
        c_b = c_g.astype(BF16)
        cb = _nt_dot(c_b, b_g.astype(BF16))
        b_t = jnp.transpose(b_g)
        s_prev = state_ref[g]
        y_off = jnp.dot(c_b, s_prev.astype(BF16), preferred_element_type=F32)
        y = jnp.zeros((lc, gw), F32)
        s_new = jnp.zeros((SSD_STATE, gw), F32)
        for r in range(heads_per_group):
            h = g * heads_per_group + r
            diff = cs_ref[:, h:h + 1] - cst_ref[h:h + 1, :]
            seg = jnp.exp(jnp.where(causal, diff, -jnp.inf))
            m_h = (cb * seg * dtt_ref[h:h + 1, :]).astype(BF16)
            y_h = jnp.dot(m_h, xs_b, preferred_element_type=F32)
            bw = (b_t * wt_ref[h:h + 1, :]).astype(BF16)
            s_h = jnp.dot(bw, xs_b, preferred_element_type=F32)
            y = jnp.where(lane_head == r, y_h + ecs_ref[:, h:h + 1] * y_off, y)
            s_new = jnp.where(lane_head_s == r, cdec_ref[:, h:h + 1] * s_prev + s_h, s_new)
        state_ref[g] = s_new
        y = y + dskip_ref[:, gc] * xs
        y = y * _silu(z_ref[:, gc])
        y = y * lax.rsqrt(jnp.mean(y * y, axis=-1, keepdims=True) + SSD_NORM_EPS)
        o_ref[:, gc] = (y * normw_ref[:, gc]).astype(o_ref.dtype)


def _ssd(proj, dt_raw, conv_w, conv_b, dt_bias_pad, a_log_pad, d_skip_ch, norm_w, batch, seq, z_col, xbc_col, width):
    t = batch * seq
    nc = seq // SSD_CHUNK
    conv_dim = conv_w.shape[1]
    gw = width // SSD_GROUPS
    row = lambda b, c: b * nc + c
    return pl.pallas_call(
        _ssd_kernel,
        grid=(batch, nc),
        in_specs=[pl.BlockSpec((SSD_CHUNK, width), lambda b, c: (row(b, c), z_col)),
                  pl.BlockSpec((SSD_CHUNK, conv_dim), lambda b, c: (row(b, c), xbc_col)),
                  pl.BlockSpec((SSD_CHUNK, LANES), lambda b, c: (row(b, c), 0)),
                  pl.BlockSpec((SSD_CONV, conv_dim), lambda b, c: (0, 0)),
                  pl.BlockSpec((1, conv_dim), lambda b, c: (0, 0)),
                  pl.BlockSpec((1, LANES), lambda b, c: (0, 0)),
                  pl.BlockSpec((1, LANES), lambda b, c: (0, 0)),
                  pl.BlockSpec((1, width), lambda b, c: (0, 0)),
                  pl.BlockSpec((1, width), lambda b, c: (0, 0))],
        out_specs=pl.BlockSpec((SSD_CHUNK, width), lambda b, c: (row(b, c), 0)),
        out_shape=jax.ShapeDtypeStruct((t, width), BF16),
        scratch_shapes=[
            pltpu.VMEM((SUBLANES + SSD_CHUNK, conv_dim), F32),
            pltpu.VMEM((SSD_CHUNK, conv_dim), F32),
            pltpu.VMEM((SSD_GROUPS, SSD_STATE, gw), F32),
            pltpu.VMEM((SSD_CHUNK, LANES), F32),
            pltpu.VMEM((SSD_CHUNK, LANES), F32),
            pltpu.VMEM((LANES, SSD_CHUNK), F32),
            pltpu.VMEM((LANES, SSD_CHUNK), F32),
            pltpu.VMEM((LANES, SSD_CHUNK), F32),
            pltpu.VMEM((1, LANES), F32),
        ],
        compiler_params=pltpu.CompilerParams(
            dimension_semantics=("parallel", "arbitrary"), vmem_limit_bytes=VMEM_LIMIT),
        name="ssd_mixer",
    )(proj, proj, dt_raw, conv_w, conv_b.reshape(1, conv_dim), dt_bias_pad, a_log_pad,
      d_skip_ch.reshape(1, width), norm_w.reshape(1, width))


def _router_kernel(x_ref, nw_ref, wr_ref, br_ref, h_ref, meta_ref, cnt_ref, carry_ref):
    tm = x_ref.shape[0]

    @pl.when(pl.program_id(0) == 0)
    def _():
        carry_ref[...] = jnp.zeros_like(carry_ref)

    x = x_ref[...]
    h = x * lax.rsqrt(jnp.mean(x * x, axis=-1, keepdims=True) + NORM_EPS) * nw_ref[...]
    h_ref[...] = h

    wp = wr_ref[...]
    h_hi = h.astype(BF16)
    rem = h - h_hi.astype(F32)
    h_mid = rem.astype(BF16)
    h_lo = (rem - h_mid.astype(F32)).astype(BF16)
    parts = (jnp.dot(h_hi, wp, preferred_element_type=F32) + jnp.dot(h_mid, wp, preferred_element_type=F32)
             + jnp.dot(h_lo, wp, preferred_element_type=F32))
    logits = parts
    for k in range(1, 3):
        logits = logits + pltpu.roll(parts, LANES - k * ROUTER_PART_LANES, 1)
    logits = logits + br_ref[...]
    lane = lax.broadcasted_iota(I32, (tm, LANES), 1)
    neg_inf = jnp.float32(-jnp.inf)

    lane_f = lane.astype(F32)

    def first_argmax(v, vmax):
        return jnp.min(jnp.where(v == vmax, lane_f, float(LANES)), axis=1, keepdims=True).astype(I32)

    lg = jnp.where(lane < N_EXPERT_GROUPS, logits, neg_inf)
    g_max = jnp.max(lg, axis=1, keepdims=True)
    g_idx = first_argmax(lg, g_max)
    g_p = 1.0 / jnp.sum(jnp.exp(lg - g_max), axis=1, keepdims=True)

    e_lo = ROUTER_EXPERT_LANE0 + EXPERTS_PER_GROUP * g_idx
    le = jnp.where((lane >= e_lo) & (lane < e_lo + EXPERTS_PER_GROUP), logits, neg_inf)
    e_max = jnp.max(le, axis=1, keepdims=True)
    i1 = first_argmax(le, e_max)
    e_sum = jnp.sum(jnp.exp(le - e_max), axis=1, keepdims=True)
    le2 = jnp.where(lane == i1, neg_inf, le)
    e_max2 = jnp.max(le2, axis=1, keepdims=True)
    i2 = first_argmax(le2, e_max2)
    p1 = 1.0 / e_sum
    p2 = jnp.exp(e_max2 - e_max) / e_sum
    gate1 = g_p * (p1 / (p1 + p2))
    gate2 = g_p * (p2 / (p1 + p2))

    hot = ((lane == i1) | (lane == i2))
    hot_b = hot.astype(BF16)
    strict = (lax.broadcasted_iota(I32, (tm, tm), 0) > lax.broadcasted_iota(I32, (tm, tm), 1)).astype(BF16)
    before = jnp.dot(strict, hot_b, preferred_element_type=F32) + carry_ref[0:1, :]
    rank1 = jnp.sum(jnp.where(lane == i1, before, 0.0), axis=1, keepdims=True)
    rank2 = jnp.sum(jnp.where(lane == i2, before, 0.0), axis=1, keepdims=True)
    carry_ref[0:1, :] = carry_ref[0:1, :] + jnp.sum(hot.astype(F32), axis=0, keepdims=True)
    cnt_ref[...] = jnp.broadcast_to(carry_ref[0:1, :], cnt_ref.shape)

    vals = [(i1 - ROUTER_EXPERT_LANE0).astype(F32), (i2 - ROUTER_EXPERT_LANE0).astype(F32), rank1, rank2, gate1, gate2]
    meta = jnp.zeros((tm, LANES), F32)
    for c, v in enumerate(vals):
        meta = jnp.where(lane == c, v, meta)
    meta_ref[...] = meta


def _router(x1, norm_w, w_router, b_router):
    t, d = x1.shape
    return pl.pallas_call(
        _router_kernel,
        grid=(t // ROUTER_TILE,),
        in_specs=[pl.BlockSpec((ROUTER_TILE, d), lambda i: (i, 0)),
                  pl.BlockSpec((1, d), lambda i: (0, 0)),
                  pl.BlockSpec((d, LANES), lambda i: (0, 0)),
                  pl.BlockSpec((1, LANES), lambda i: (0, 0))],
        out_specs=[pl.BlockSpec((ROUTER_TILE, d), lambda i: (i, 0)),
                   pl.BlockSpec((ROUTER_TILE, LANES), lambda i: (i, 0)),
                   pl.BlockSpec((SUBLANES, LANES), lambda i: (0, 0))],
        out_shape=[jax.ShapeDtypeStruct((t, d), F32),
                   jax.ShapeDtypeStruct((t, LANES), F32),
                   jax.ShapeDtypeStruct((SUBLANES, LANES), F32)],
        scratch_shapes=[pltpu.VMEM((SUBLANES, LANES), F32)],
        compiler_params=pltpu.CompilerParams(dimension_semantics=("arbitrary",), vmem_limit_bytes=VMEM_LIMIT),
        name="router",
    )(x1, norm_w.reshape(1, d), w_router, b_router)


def _wait_rows(src_ref, dst_ref, n, sem):
    done = 0
    for rows in (64, SUBLANES, 1):
        trips = lax.shift_right_logical(n - done, rows.bit_length() - 1)

        def body(r, c, rows=rows):
            pltpu.make_async_copy(src_ref.at[pl.ds(0, rows)], dst_ref.at[pl.ds(0, rows)], sem).wait()
            return c
        lax.fori_loop(0, trips, body, 0)
        done = done + trips * rows


def _for_rows(n, body):
    n_groups = lax.shift_right_logical(n, ROW_UNROLL.bit_length() - 1)

    def group(g, c):
        base = pl.multiple_of(g * ROW_UNROLL, ROW_UNROLL)
        for u in range(ROW_UNROLL):
            body(base + u, lambda buf, u=u: buf.at[pl.ds(base, ROW_UNROLL)].at[pl.ds(u, 1)])
        return c
    lax.fori_loop(0, n_groups, group, 0)

    def tail(r, c):
        body(r, lambda buf: buf.at[pl.ds(r, 1)])
        return c
    lax.fori_loop(n_groups * ROW_UNROLL, n, tail, 0)


def _for_row_blocks(n_rows, fn):
    n_blk = lax.shift_right_logical(n_rows + (MOE_SUB - 1), MOE_SUB.bit_length() - 1)
    for blocks in range(1, MOE_ROWS // MOE_SUB + 1):
        pl.when(n_blk == blocks)(functools.partial(fn, blocks * MOE_SUB))


def _moe_up_kernel(n_items_ref, item_e_ref, item_start_ref, item_n_ref, flat_ref,
                   h_hbm, wg_ref, wu_ref, o_ref, xbuf_ref, xb_ref, sem, *, n_tokens):
    i = pl.program_id(0)
    j = pl.program_id(1)
    n_items = n_items_ref[0]
    valid = i < n_items
    conv_rows = 64

    def start_gather(item):
        start = item_start_ref[item]

        def body(r, row_of):
            tok = flat_ref[start + r] & (n_tokens - 1)
            pltpu.make_async_copy(h_hbm.at[pl.ds(tok, 1)], row_of(xbuf_ref), sem).start()
        _for_rows(item_n_ref[item], body)

    @pl.when((i == 0) & (j == 0))
    def _():
        xbuf_ref[...] = jnp.zeros_like(xbuf_ref)
        start_gather(0)

    @pl.when(valid & (j == 0))
    def _():
        _wait_rows(h_hbm, xbuf_ref, item_n_ref[i], sem)

        def convert(s, c):
            rows = pl.ds(pl.multiple_of(s * conv_rows, conv_rows), conv_rows)
            xb_ref[rows, :] = xbuf_ref[rows, :].astype(BF16)
            return c
        lax.fori_loop(0, MOE_ROWS // conv_rows, convert, 0)

        @pl.when(i + 1 < n_items)
        def _():
            start_gather(i + 1)

    o_ref[...] = jnp.zeros_like(o_ref)

    def compute(m):
        xs = xb_ref[0:m, :]
        g = jnp.dot(xs, wg_ref[...].astype(BF16), preferred_element_type=F32)
        u = jnp.dot(xs, wu_ref[...].astype(BF16), preferred_element_type=F32)
        o_ref[0:m, :] = (_silu(g) * u).astype(o_ref.dtype)

    @pl.when(valid)
    def _():
        _for_row_blocks(item_n_ref[i], compute)


def _moe_up(sched, h2, w_gate, w_up, n_items_max):
    n_tokens, _ = h2.shape
    _, d, ff = w_gate.shape
    nj = ff // MOE_FF_TILE

    def w_map(i, j, n_items, item_e, *_):
        ii = jnp.clip(i, 0, jnp.maximum(n_items[0] - 1, 0))
        return (item_e[ii], 0, jnp.where(i < n_items[0], j, nj - 1))

    def o_map(i, j, *_):
        return (i, j)

    return pl.pallas_call(
        functools.partial(_moe_up_kernel, n_tokens=n_tokens),
        grid_spec=pltpu.PrefetchScalarGridSpec(
            num_scalar_prefetch=5,
            grid=(n_items_max, nj),
            in_specs=[pl.BlockSpec(memory_space=pl.ANY),
                      pl.BlockSpec((None, d, MOE_FF_TILE), w_map),
                      pl.BlockSpec((None, d, MOE_FF_TILE), w_map)],
            out_specs=pl.BlockSpec((MOE_ROWS, MOE_FF_TILE), o_map),
            scratch_shapes=[pltpu.VMEM((MOE_ROWS, d), F32),
                            pltpu.VMEM((MOE_ROWS, d), BF16),
                            pltpu.SemaphoreType.DMA(())]),
        out_shape=jax.ShapeDtypeStruct((n_items_max * MOE_ROWS, ff), BF16),
        compiler_params=pltpu.CompilerParams(
            dimension_semantics=("arbitrary", "arbitrary"), vmem_limit_bytes=VMEM_LIMIT),
        name="moe_up",
    )(*sched, h2, w_gate, w_up)


def _moe_down_kernel(n_items_ref, item_e_ref, item_start_ref, item_n_ref, flat_ref,
                     h_ref, wd_ref, y_hbm, *scratch):
    i = pl.program_id(0)
    j = pl.program_id(1)
    valid = i < n_items_ref[0]
    tn = wd_ref.shape[1]
    ybufs = scratch[:MOE_DOWN_SPLIT]
    sems = scratch[MOE_DOWN_SPLIT:2 * MOE_DOWN_SPLIT]
    pending_ref = scratch[2 * MOE_DOWN_SPLIT]

    @pl.when((i == 0) & (j == 0))
    def _():
        for jj in range(MOE_DOWN_SPLIT):
            pending_ref[jj] = 0

    def drain(jj):
        y_part = y_hbm.at[:, pl.ds(jj * tn, tn)]
        _wait_rows(ybufs[jj], y_part, pending_ref[jj], sems[jj])
        pending_ref[jj] = 0

    for jj in range(MOE_DOWN_SPLIT):
        @pl.when(valid & (j == jj))
        def _(jj=jj):
            ybuf = ybufs[jj]
            y_part = y_hbm.at[:, pl.ds(jj * tn, tn)]
            drain(jj)
            n = item_n_ref[i]
            start = item_start_ref[i]

            def compute(m):
                ybuf[0:m, :] = jnp.dot(h_ref[0:m, :], wd_ref[...].astype(BF16), preferred_element_type=F32)
            _for_row_blocks(n, compute)

            def issue(r, row_of):
                dest = flat_ref[start + r]
                pltpu.make_async_copy(row_of(ybuf), y_part.at[pl.ds(dest, 1)], sems[jj]).start()
            _for_rows(n, issue)
            pending_ref[jj] = n

    @pl.when((i == pl.num_programs(0) - 1) & (j == MOE_DOWN_SPLIT - 1))
    def _():
        for jj in range(MOE_DOWN_SPLIT):
            drain(jj)


def _moe_down(sched, h_items, w_down, n_items_max, n_tokens):
    _, ff, d = w_down.shape
    tn = d // MOE_DOWN_SPLIT

    def w_map(i, j, n_items, item_e, *_):
        ii = jnp.clip(i, 0, jnp.maximum(n_items[0] - 1, 0))
        return (item_e[ii], 0, jnp.where(i < n_items[0], j, MOE_DOWN_SPLIT - 1))

    def h_map(i, j, n_items, *_):
        return (jnp.clip(i, 0, jnp.maximum(n_items[0] - 1, 0)), 0)

    return pl.pallas_call(
        _moe_down_kernel,
        grid_spec=pltpu.PrefetchScalarGridSpec(
            num_scalar_prefetch=5,
            grid=(n_items_max, MOE_DOWN_SPLIT),
            in_specs=[pl.BlockSpec((MOE_ROWS, ff), h_map),
                      pl.BlockSpec((None, ff, tn), w_map)],
            out_specs=pl.BlockSpec(memory_space=pl.ANY),
            scratch_shapes=([pltpu.VMEM((MOE_ROWS, tn), F32)] * MOE_DOWN_SPLIT
                            + [pltpu.SemaphoreType.DMA(())] * MOE_DOWN_SPLIT
                            + [pltpu.SMEM((MOE_DOWN_SPLIT,), I32)])),
        out_shape=jax.ShapeDtypeStruct((TOP_K * n_tokens, d), F32),
        compiler_params=pltpu.CompilerParams(
            dimension_semantics=("arbitrary", "arbitrary"), vmem_limit_bytes=VMEM_LIMIT),
        name="moe_down",
    )(*sched, h_items, w_down)


def _combine_kernel(x_ref, meta_ref, y0_ref, y1_ref, o_ref):
    o_ref[...] = x_ref[...] + meta_ref[:, 4:5] * y0_ref[...] + meta_ref[:, 5:6] * y1_ref[...]


def _combine(x1, meta, y_slots):
    t, d = x1.shape
    tm = ROUTER_TILE
    nt = t // tm
    return pl.pallas_call(
        _combine_kernel,
        grid=(nt,),
        in_specs=[pl.BlockSpec((tm, d), lambda i: (i, 0)),
                  pl.BlockSpec((tm, LANES), lambda i: (i, 0)),
                  pl.BlockSpec((tm, d), lambda i: (i, 0)),
                  pl.BlockSpec((tm, d), lambda i: (nt + i, 0))],
        out_specs=pl.BlockSpec((tm, d), lambda i: (i, 0)),
        out_shape=jax.ShapeDtypeStruct((t, d), F32),
        compiler_params=pltpu.CompilerParams(dimension_semantics=("parallel",), vmem_limit_bytes=VMEM_LIMIT),
        name="moe_combine",
    )(x1, meta, y_slots, y_slots)


def _moe_schedule(meta, counts_f, n_tokens, n_items_max):
    ids = jnp.transpose(meta[:, 0:2 * TOP_K]).astype(I32)
    e_id = ids[0:TOP_K]
    rank = ids[TOP_K:2 * TOP_K]
    counts = counts_f[0, ROUTER_EXPERT_LANE0:ROUTER_EXPERT_LANE0 + N_EXPERTS].astype(I32)
    starts = jnp.cumsum(counts) - counts
    experts = jnp.arange(N_EXPERTS, dtype=I32)[:, None, None]
    start_of = jnp.sum(jnp.where(e_id[None] == experts, starts[:, None, None], 0), axis=0)
    pos = start_of + rank
    flat = jnp.arange(TOP_K * n_tokens, dtype=I32)
    flat_sorted = jnp.zeros((TOP_K * n_tokens,), I32).at[pos.reshape(-1)].set(flat, unique_indices=True)
    chunks = (counts + MOE_ROWS - 1) // MOE_ROWS
    chunk_end = jnp.cumsum(chunks)
    n_items = chunk_end[-1]
    item = jnp.arange(n_items_max, dtype=I32)
    item_e = jnp.minimum(jnp.sum(item[:, None] >= chunk_end[None, :], axis=1), N_EXPERTS - 1).astype(I32)
    local = item - (chunk_end - chunks)[item_e]
    item_start = starts[item_e] + local * MOE_ROWS
    item_n = jnp.clip(counts[item_e] - local * MOE_ROWS, 0, MOE_ROWS)
    in_range = item < n_items
    item_start = jnp.where(in_range, item_start, 0).astype(I32)
    item_n = jnp.where(in_range, item_n, 0).astype(I32)
    return (n_items.reshape(1).astype(I32), item_e, item_start, item_n, flat_sorted)


def _layer(x, positions, norm_attn_w, w_in, q_norm_w, k_norm_w, conv_w, conv_b, dt_bias, a_log, d_skip,
           ssd_norm_w, w_out, norm_ffn_w, router_group_w, router_group_b, router_expert_w, router_expert_b,
           w_gate, w_up, w_down):
    batch, seq, d = x.shape
    t = batch * seq
    attn_width = d // 2
    n_heads = attn_width // HEAD_DIM
    ssd_width = d - attn_width
    ssd_heads = ssd_width // SSD_HEAD_DIM
    conv_dim = ssd_width + 2 * SSD_GROUPS * SSD_STATE
    main_cols = 3 * attn_width + ssd_width + conv_dim
    assert w_in.shape[1] == main_cols + ssd_heads and ssd_heads <= LANES
    assert seq % (SSD_CHUNK) == 0 and t % IN_TM == 0 and (t & (t - 1)) == 0

    x2d = x.reshape(t, d)
    h = _rmsnorm_cast(x2d, norm_attn_w)
    wt_in = jnp.transpose(w_in)
    proj = _matmul_nt(h, wt_in, main_cols, IN_TM, IN_TN, "in_proj")
    wt_dt = jnp.pad(wt_in[main_cols:, :], ((0, LANES - ssd_heads), (0, 0)))
    dt_raw = _matmul_nt(h, wt_dt, LANES, IN_TM, LANES, "dt_proj")

    half = HEAD_DIM // 2
    inv_freq = jnp.power(jnp.float32(ROPE_THETA), -jnp.arange(half, dtype=F32) / half)
    rope_tab = jnp.stack([jnp.concatenate([inv_freq, inv_freq]),
                          jnp.concatenate([-jnp.ones((half,), F32), jnp.ones((half,), F32)])])
    attn = _attention(proj, positions.reshape(t, 1), rope_tab, q_norm_w, k_norm_w, batch, seq, n_heads)

    pad_heads = lambda v: jnp.pad(v.astype(F32), (0, LANES - ssd_heads)).reshape(1, LANES)
    ssd = _ssd(proj, dt_raw, conv_w, conv_b, pad_heads(dt_bias), pad_heads(a_log),
               jnp.repeat(d_skip.astype(F32), SSD_HEAD_DIM), ssd_norm_w, batch, seq,
               z_col=(3 * attn_width) // ssd_width, xbc_col=(3 * attn_width + ssd_width) // conv_dim,
               width=ssd_width)

    x1 = _out_proj(attn, ssd, w_out, x2d, OUT_TM, OUT_TN)

    n_router = N_EXPERT_GROUPS + N_EXPERTS
    assert n_router <= ROUTER_PART_LANES and 3 * ROUTER_PART_LANES <= LANES
    w_cat = jnp.concatenate([router_group_w, router_expert_w], axis=1).astype(F32)
    w_parts, rem = [], w_cat
    for _ in range(3):
        part = rem.astype(BF16)
        w_parts.append(jnp.pad(part, ((0, 0), (0, ROUTER_PART_LANES - n_router))))
        rem = rem - part.astype(F32)
    w_router = jnp.pad(jnp.concatenate(w_parts, axis=1), ((0, 0), (0, LANES - 3 * ROUTER_PART_LANES)))
    b_router = jnp.pad(jnp.concatenate([router_group_b, router_expert_b]), (0, LANES - n_router)).reshape(1, LANES)
    h2, meta, counts_f = _router(x1, norm_ffn_w, w_router, b_router)

    n_items_max = N_EXPERTS + (TOP_K * t) // MOE_ROWS
    sched = _moe_schedule(meta, counts_f, t, n_items_max)
    h_items = _moe_up(sched, h2, w_gate, w_up, n_items_max)
    y_slots = _moe_down(sched, h_items, w_down, n_items_max, t)
    out = _combine(x1, meta, y_slots)
    return out.reshape(batch, seq, d)


def kernel(x, positions, norm_attn_w, w_in, q_norm_w, k_norm_w, conv_w, conv_b, dt_bias, a_log, d_skip, ssd_norm_w, w_out, norm_ffn_w, router_group_w, router_group_b, router_expert_w, router_expert_b, w_gate, w_up, w_down):
    for layer in range(norm_attn_w.shape[0]):
        x = _layer(x, positions, norm_attn_w[layer], w_in[layer], q_norm_w[layer], k_norm_w[layer],
                   conv_w[layer], conv_b[layer], dt_bias[layer], a_log[layer], d_skip[layer],
                   ssd_norm_w[layer], w_out[layer], norm_ffn_w[layer], router_group_w[layer],
                   router_group_b[layer], router_expert_w[layer], router_expert_b[layer],
                   w_gate[layer], w_up[layer], w_down[layer])
    return x
```

```python
import functools

import jax
import jax.numpy as jnp
from jax import lax
from jax.experimental import pallas as pl
from jax.experimental.pallas import tpu as pltpu

F32 = jnp.float32
BF16 = jnp.bfloat16
I32 = jnp.int32
HIGHEST = lax.Precision.HIGHEST

HEAD_DIM = 128
DILATED_BRANCHES = ((128, 1), (512, 4), (2048, 16))
ATTN_BLOCK = 128
ROPE_THETA = 10000.0
SSD_HEAD_DIM = 64
SSD_GROUPS = 8
SSD_STATE = 128
SSD_CONV = 4
SSD_CHUNK = 256
N_EXPERT_GROUPS = 4
EXPERTS_PER_GROUP = 8
N_EXPERTS = N_EXPERT_GROUPS * EXPERTS_PER_GROUP
TOP_K = 2
NORM_EPS = 1e-6
SSD_NORM_EPS = 1e-5

LANES = 128
SUBLANES = 8
VMEM_LIMIT = 56 * 1024 * 1024

ROW_TILE = 512
IN_TM, IN_TN = 1024, 512
OUT_TM, OUT_TN = 1024, 512
MOE_ROWS = 1024
MOE_SUB = 128
ROW_UNROLL = 8
MOE_FF_TILE = 256
MOE_DOWN_SPLIT = 2
ROUTER_TILE = 256
ROUTER_EXPERT_LANE0 = N_EXPERT_GROUPS
ROUTER_PART_LANES = 40


def _silu(v):
    return v * (1.0 / (1.0 + jnp.exp(-v)))


def _nt_dot(a, b, **kw):
    return lax.dot_general(a, b, (((1,), (1,)), ((), ())), preferred_element_type=F32, **kw)


def _rmsnorm_cast_kernel(x_ref, w_ref, o_ref):
    x = x_ref[...]
    y = x * lax.rsqrt(jnp.mean(x * x, axis=-1, keepdims=True) + NORM_EPS)
    o_ref[...] = (y * w_ref[...]).astype(o_ref.dtype)


def _rmsnorm_cast(x2d, w):
    t, d = x2d.shape
    return pl.pallas_call(
        _rmsnorm_cast_kernel,
        grid=(t // ROW_TILE,),
        in_specs=[pl.BlockSpec((ROW_TILE, d), lambda i: (i, 0)),
                  pl.BlockSpec((1, d), lambda i: (0, 0))],
        out_specs=pl.BlockSpec((ROW_TILE, d), lambda i: (i, 0)),
        out_shape=jax.ShapeDtypeStruct((t, d), BF16),
        compiler_params=pltpu.CompilerParams(dimension_semantics=("parallel",), vmem_limit_bytes=VMEM_LIMIT),
        name="rmsnorm_cast",
    )(x2d, w.reshape(1, d))


def _matmul_nt_kernel(a_ref, wt_ref, o_ref):
    o_ref[...] = _nt_dot(a_ref[...], wt_ref[...].astype(BF16)).astype(o_ref.dtype)


def _matmul_nt(a, wt, n_rows, tm, tn, name):
    m, kdim = a.shape
    return pl.pallas_call(
        _matmul_nt_kernel,
        grid=(m // tm, n_rows // tn),
        in_specs=[pl.BlockSpec((tm, kdim), lambda i, j: (i, 0)),
                  pl.BlockSpec((tn, kdim), lambda i, j: (j, 0))],
        out_specs=pl.BlockSpec((tm, tn), lambda i, j: (i, j)),
        out_shape=jax.ShapeDtypeStruct((m, n_rows), F32),
        compiler_params=pltpu.CompilerParams(
            dimension_semantics=("parallel", "parallel"), vmem_limit_bytes=VMEM_LIMIT),
        name=name,
    )(a, wt)


def _out_proj_kernel(a1_ref, a2_ref, w_ref, x_ref, o_ref):
    k1 = a1_ref.shape[1]
    acc = jnp.dot(a1_ref[...], w_ref[0:k1, :].astype(BF16), preferred_element_type=F32)
    acc = acc + jnp.dot(a2_ref[...], w_ref[k1:, :].astype(BF16), preferred_element_type=F32)
    o_ref[...] = x_ref[...] + acc


def _out_proj(a1, a2, w, x2d, tm, tn):
    m, k1 = a1.shape
    k2 = a2.shape[1]
    kdim, n = w.shape
    return pl.pallas_call(
        _out_proj_kernel,
        grid=(m // tm, n // tn),
        in_specs=[pl.BlockSpec((tm, k1), lambda i, j: (i, 0)),
                  pl.BlockSpec((tm, k2), lambda i, j: (i, 0)),
                  pl.BlockSpec((kdim, tn), lambda i, j: (0, j)),
                  pl.BlockSpec((tm, tn), lambda i, j: (i, j))],
        out_specs=pl.BlockSpec((tm, tn), lambda i, j: (i, j)),
        out_shape=jax.ShapeDtypeStruct((m, n), F32),
        compiler_params=pltpu.CompilerParams(
            dimension_semantics=("parallel", "parallel"), vmem_limit_bytes=VMEM_LIMIT),
        name="out_proj",
    )(a1, a2, w, x2d)


def _attn_kernel(pos_ref, rope_ref, qw_ref, kw_ref, q_ref, k_ref, v_ref, o_ref,
                 cos_ref, sin_ref, rms_all_ref, qf_ref, kf_ref, qd_all_ref, kd_all_ref, vd_all_ref,
                 bias_all_ref, s_all_ref, p_all_ref, m_all_ref, o_br_ref, lse_br_ref, od_all_ref, lsed_all_ref):
    seq = q_ref.shape[0]
    blk = ATTN_BLOCK
    n_blocks = seq // blk

    @pl.when(pl.program_id(1) == 0)
    def _():
        ang = pos_ref[...].astype(F32) * rope_ref[0:1, :]
        cos_ref[...] = jnp.cos(ang)
        sin_ref[...] = jnp.sin(ang) * rope_ref[1:2, :]

    def norm_rot(t_ref, w_ref, dst_ref, rms_ref, scale):
        def inv_rms(i, c):
            rows = pl.ds(pl.multiple_of(i * 256, 256), 256)
            t = t_ref[rows, :]
            r = lax.rsqrt(jnp.mean(t * t, axis=-1, keepdims=True) + NORM_EPS)
            rms_ref[rows, :] = jnp.broadcast_to(r, (256, HEAD_DIM))
            return c
        lax.fori_loop(0, seq // 256, inv_rms, 0, unroll=True)

        def rotate(i, c):
            rows = pl.ds(pl.multiple_of(i * 256, 256), 256)
            y = t_ref[rows, :] * rms_ref[rows, :] * w_ref[...]
            y = y * cos_ref[rows, :] + pltpu.roll(y, HEAD_DIM // 2, 1) * sin_ref[rows, :]
            dst_ref[rows, :] = y * scale if scale != 1.0 else y
            return c
        lax.fori_loop(0, seq // 256, rotate, 0, unroll=True)

    norm_rot(q_ref, qw_ref, qf_ref, rms_all_ref.at[0], HEAD_DIM ** -0.5)
    norm_rot(k_ref, kw_ref, kf_ref, rms_all_ref.at[1], 1.0)

    for bi, (window, dil) in enumerate(DILATED_BRANCHES):
        qd_ref, kd_ref, vd_ref = qd_all_ref.at[bi], kd_all_ref.at[bi], vd_all_ref.at[bi]
        bias_ref, s_ref, p_ref, m_ref = bias_all_ref.at[bi], s_all_ref.at[bi], p_all_ref.at[bi], m_all_ref.at[bi]
        od_ref, lsed_ref = od_all_ref.at[bi], lsed_all_ref.at[bi]
        kd_ref[0:blk, :] = jnp.zeros((blk, HEAD_DIM), BF16)
        vd_ref[0:blk, 0:HEAD_DIM] = jnp.zeros((blk, HEAD_DIM), BF16)
        vd_ref[:, HEAD_DIM:] = jnp.ones((blk + seq, HEAD_DIM), BF16)
        n_back = window // dil
        sub_len = seq // dil
        nb = sub_len // blk
        natural = dil == 1
        kw = blk if nb == 1 else 2 * blk
        k_off = blk if nb == 1 else 0
        for r in range(dil):
            src = slice(None) if natural else pl.ds(r, sub_len, stride=dil)
            dst = slice(r * sub_len, (r + 1) * sub_len)
            dstp = slice(blk + r * sub_len, blk + (r + 1) * sub_len)
            qd_ref[dst, :] = qf_ref[src, :].astype(BF16)
            kd_ref[dstp, :] = kf_ref[src, :].astype(BF16)
            vd_ref[dstp, 0:HEAD_DIM] = v_ref[src, :].astype(BF16)

        o_out = o_br_ref.at[bi] if natural else od_ref
        lse_out = lse_br_ref.at[bi] if natural else lsed_ref

        qi = lax.broadcasted_iota(I32, (blk, kw), 0)
        kj = lax.broadcasted_iota(I32, (blk, kw), 1)
        dist = (blk - k_off) + qi - kj
        ok = (dist >= 0) & (dist <= n_back)
        bias_ref[0, :, 0:kw] = jnp.where(ok, 0.0, -jnp.inf)
        bias_ref[1, :, 0:kw] = jnp.where(ok & (kj >= blk - k_off), 0.0, -jnp.inf)

        def scores(j, c):
            r0 = pl.multiple_of(j * blk, blk)
            s_ref[pl.ds(r0, blk), 0:kw] = _nt_dot(qd_ref[pl.ds(r0, blk), :], kd_ref[pl.ds(r0 + k_off, kw), :])
            return c
        lax.fori_loop(0, n_blocks, scores, 0, unroll=True)

        def softmax(j, c):
            r0 = pl.multiple_of(j * blk, blk)
            first = jnp.where((j & (nb - 1)) == 0, 1, 0)
            s = s_ref[pl.ds(r0, blk), 0:kw] + bias_ref[first, :, 0:kw]
            m = jnp.max(s, axis=1, keepdims=True)
            p_ref[pl.ds(r0, blk), 0:kw] = jnp.exp(s - m).astype(BF16)
            m_ref[pl.ds(r0, blk), :] = jnp.broadcast_to(m, (blk, HEAD_DIM))
            return c
        lax.fori_loop(0, n_blocks, softmax, 0, unroll=8)

        def weighted(j, c):
            r0 = pl.multiple_of(j * blk, blk)
            acc = jnp.dot(p_ref[pl.ds(r0, blk), 0:kw], vd_ref[pl.ds(r0 + k_off, kw), :], preferred_element_type=F32)
            l = acc[:, HEAD_DIM:]
            o_out[pl.ds(r0, blk), :] = acc[:, 0:HEAD_DIM] / l
            lse_out[pl.ds(r0, blk), :] = m_ref[pl.ds(r0, blk), :] + jnp.log(l)
            return c
        lax.fori_loop(0, n_blocks, weighted, 0, unroll=True)

        if not natural:
            for r in range(dil):
                src = slice(r * sub_len, (r + 1) * sub_len)
                dst = pl.ds(r, sub_len, stride=dil)
                o_br_ref[bi, dst, :] = od_ref[src, :]
                lse_br_ref[bi, dst, :] = lsed_ref[src, :]

    n_br = len(DILATED_BRANCHES)

    def merge(i, c):
        rows = pl.ds(pl.multiple_of(i * 256, 256), 256)
        lses = [lse_br_ref[b, rows, :] for b in range(n_br)]
        top = functools.reduce(jnp.maximum, lses)
        num = jnp.zeros((256, HEAD_DIM), F32)
        den = jnp.zeros((256, HEAD_DIM), F32)
        for b in range(n_br):
            e = jnp.exp(lses[b] - top)
            num = num + e * o_br_ref[b, rows, :]
            den = den + e
        o_ref[rows, :] = (num / den).astype(o_ref.dtype)
        return c
    lax.fori_loop(0, seq // 256, merge, 0)


def _attention(proj, pos_col, rope_tab, q_norm_w, k_norm_w, batch, seq, n_heads):
    t = batch * seq
    nbr = len(DILATED_BRANCHES)
    qkv_spec = lambda off: pl.BlockSpec((seq, HEAD_DIM), lambda b, h: (b, off + h))
    return pl.pallas_call(
        _attn_kernel,
        grid=(batch, n_heads),
        in_specs=[pl.BlockSpec((seq, 1), lambda b, h: (b, 0)),
                  pl.BlockSpec((2, HEAD_DIM), lambda b, h: (0, 0)),
                  pl.BlockSpec((1, HEAD_DIM), lambda b, h: (0, 0)),
                  pl.BlockSpec((1, HEAD_DIM), lambda b, h: (0, 0)),
                  qkv_spec(0), qkv_spec(n_heads), qkv_spec(2 * n_heads)],
        out_specs=pl.BlockSpec((seq, HEAD_DIM), lambda b, h: (b, h)),
        out_shape=jax.ShapeDtypeStruct((t, n_heads * HEAD_DIM), BF16),
        scratch_shapes=[
            pltpu.VMEM((seq, HEAD_DIM), F32),
            pltpu.VMEM((seq, HEAD_DIM), F32),
            pltpu.VMEM((2, seq, HEAD_DIM), F32),
            pltpu.VMEM((seq, HEAD_DIM), F32),
            pltpu.VMEM((seq, HEAD_DIM), F32),
            pltpu.VMEM((nbr, seq, HEAD_DIM), BF16),
            pltpu.VMEM((nbr, ATTN_BLOCK + seq, HEAD_DIM), BF16),
            pltpu.VMEM((nbr, ATTN_BLOCK + seq, 2 * HEAD_DIM), BF16),
            pltpu.VMEM((nbr, 2, ATTN_BLOCK, 2 * ATTN_BLOCK), F32),
            pltpu.VMEM((nbr, seq, 2 * ATTN_BLOCK), F32),
            pltpu.VMEM((nbr, seq, 2 * ATTN_BLOCK), BF16),
            pltpu.VMEM((nbr, seq, HEAD_DIM), F32),
            pltpu.VMEM((nbr, seq, HEAD_DIM), F32),
            pltpu.VMEM((nbr, seq, HEAD_DIM), F32),
            pltpu.VMEM((nbr, seq, HEAD_DIM), F32),
            pltpu.VMEM((nbr, seq, HEAD_DIM), F32),
        ],
        compiler_params=pltpu.CompilerParams(
            dimension_semantics=("parallel", "arbitrary"), vmem_limit_bytes=VMEM_LIMIT),
        name="dilated_attention",
    )(pos_col, rope_tab, q_norm_w.reshape(1, HEAD_DIM), k_norm_w.reshape(1, HEAD_DIM), proj, proj, proj)


def _ssd_kernel(z_ref, xbc_ref, dt_ref, convw_ref, convb_ref, dtb_ref, alog_ref, dskip_ref, normw_ref,
                o_ref, ext_ref, act_ref, state_ref, cs_ref, ecs_ref, cst_ref, wt_ref, dtt_ref, cdec_ref):
    lc = SSD_CHUNK
    width = o_ref.shape[1]
    gw = width // SSD_GROUPS
    heads_per_group = gw // SSD_HEAD_DIM
    conv_dim = xbc_ref.shape[1]
    halo = SUBLANES

    @pl.when(pl.program_id(1) == 0)
    def _():
        ext_ref[0:halo, :] = jnp.zeros((halo, conv_dim), F32)
        state_ref[...] = jnp.zeros_like(state_ref)

    ext_ref[halo:halo + lc, :] = xbc_ref[...]
    col_tile = 512
    for ct in range(conv_dim // col_tile):
        cols = slice(ct * col_tile, (ct + 1) * col_tile)
        acc = jnp.broadcast_to(convb_ref[:, cols], (lc, col_tile))
        for j in range(SSD_CONV):
            r0 = halo - (SSD_CONV - 1) + j
            acc = acc + convw_ref[j:j + 1, cols] * ext_ref[r0:r0 + lc, cols]
        act_ref[:, cols] = _silu(acc)
    ext_ref[0:halo, :] = xbc_ref[lc - halo:lc, :]

    x0 = dt_ref[...] + dtb_ref[...]
    dt = jnp.maximum(x0, 0.0) + jnp.log1p(jnp.exp(-jnp.abs(x0)))
    d_a = dt * (-jnp.exp(alog_ref[...]))
    tri = (lax.broadcasted_iota(I32, (lc, lc), 0) >= lax.broadcasted_iota(I32, (lc, lc), 1)).astype(F32)
    cs = jnp.dot(tri, d_a, precision=HIGHEST, preferred_element_type=F32)
    cs_last = cs[lc - 1:lc, :]
    cs_ref[...] = cs
    ecs_ref[...] = jnp.exp(cs)
    cdec_ref[...] = jnp.exp(cs_last)
    eye = (lax.broadcasted_iota(I32, (LANES, LANES), 0) == lax.broadcasted_iota(I32, (LANES, LANES), 1)).astype(F32)
    cst_ref[...] = _nt_dot(eye, cs, precision=HIGHEST)
    wt_ref[...] = _nt_dot(eye, jnp.exp(cs_last - cs) * dt, precision=HIGHEST)
    dtt_ref[...] = _nt_dot(eye, dt, precision=HIGHEST)

    causal = lax.broadcasted_iota(I32, (lc, lc), 0) >= lax.broadcasted_iota(I32, (lc, lc), 1)
    lane_head = lax.broadcasted_iota(I32, (lc, gw), 1) // SSD_HEAD_DIM
    lane_head_s = lax.broadcasted_iota(I32, (SSD_STATE, gw), 1) // SSD_HEAD_DIM
    n_xs = width
    n_b = SSD_GROUPS * SSD_STATE

    for g in range(SSD_GROUPS):
        gc = slice(g * gw, (g + 1) * gw)
        xs = act_ref[:, gc]
        b_g = act_ref[:, n_xs + g * SSD_STATE:n_xs + (g + 1) * SSD_STATE]
        c_g = act_ref[:, n_xs + n_b + g * SSD_STATE:n_xs + n_b + (g + 1) * SSD_STATE]
        xs_b = xs.astype(BF16)
        c_b = c_g.astype(BF16)
        cb = _nt_dot(c_b, b_g.astype(BF16))
        b_t = jnp.transpose(b_g)
        s_prev = state_ref[g]
        y_off = jnp.dot(c_b, s_prev.astype(BF16), preferred_element_type=F32)
        y = jnp.zeros((lc, gw), F32)
        s_new = jnp.zeros((SSD_STATE, gw), F32)
        for r in range(heads_per_group):
            h = g * heads_per_group + r
            diff = cs_ref[:, h:h + 1] - cst_ref[h:h + 1, :]
            seg = jnp.exp(jnp.where(causal, diff, -jnp.inf))
            m_h = (cb * seg * dtt_ref[h:h + 1, :]).astype(BF16)
            y_h = jnp.dot(m_h, xs_b, preferred_element_type=F32)
            bw = (b_t * wt_ref[h:h + 1, :]).astype(BF16)
            s_h = jnp.dot(bw, xs_b, preferred_element_type=F32)
            y = jnp.where(lane_head == r, y_h + ecs_ref[:, h:h + 1] * y_off, y)
            s_new = jnp.where(lane_head_s == r, cdec_ref[:, h:h + 1] * s_prev + s_h, s_new)
        state_ref[g] = s_new
        y = y + dskip_ref[:, gc] * xs
        y = y * _silu(z_ref[:, gc])
        y = y * lax.rsqrt(jnp.mean(y * y, axis=-1, keepdims=True) + SSD_NORM_EPS)
        o_ref[:, gc] = (y * normw_ref[:, gc]).astype(o_ref.dtype)


def _ssd(proj, dt_raw, conv_w, conv_b, dt_bias_pad, a_log_pad, d_skip_ch, norm_w, batch, seq, z_col, xbc_col, width):
    t = batch * seq
    nc = seq // SSD_CHUNK
    conv_dim = conv_w.shape[1]
    gw = width // SSD_GROUPS
    row = lambda b, c: b * nc + c
    return pl.pallas_call(
        _ssd_kernel,
        grid=(batch, nc),
        in_specs=[pl.BlockSpec((SSD_CHUNK, width), lambda b, c: (row(b, c), z_col)),
                  pl.BlockSpec((SSD_CHUNK, conv_dim), lambda b, c: (row(b, c), xbc_col)),
                  pl.BlockSpec((SSD_CHUNK, LANES), lambda b, c: (row(b, c), 0)),
                  pl.BlockSpec((SSD_CONV, conv_dim), lambda b, c: (0, 0)),
                  pl.BlockSpec((1, conv_dim), lambda b, c: (0, 0)),
                  pl.BlockSpec((1, LANES), lambda b, c: (0, 0)),
                  pl.BlockSpec((1, LANES), lambda b, c: (0, 0)),
                  pl.BlockSpec((1, width), lambda b, c: (0, 0)),
                  pl.BlockSpec((1, width), lambda b, c: (0, 0))],
        out_specs=pl.BlockSpec((SSD_CHUNK, width), lambda b, c: (row(b, c), 0)),
        out_shape=jax.ShapeDtypeStruct((t, width), BF16),
        scratch_shapes=[
            pltpu.VMEM((SUBLANES + SSD_CHUNK, conv_dim), F32),
            pltpu.VMEM((SSD_CHUNK, conv_dim), F32),
            pltpu.VMEM((SSD_GROUPS, SSD_STATE, gw), F32),
            pltpu.VMEM((SSD_CHUNK, LANES), F32),
            pltpu.VMEM((SSD_CHUNK, LANES), F32),
            pltpu.VMEM((LANES, SSD_CHUNK), F32),
            pltpu.VMEM((LANES, SSD_CHUNK), F32),
            pltpu.VMEM((LANES, SSD_CHUNK), F32),
            pltpu.VMEM((1, LANES), F32),
        ],
        compiler_params=pltpu.CompilerParams(
            dimension_semantics=("parallel", "arbitrary"), vmem_limit_bytes=VMEM_LIMIT),
        name="ssd_mixer",
    )(proj, proj, dt_raw, conv_w, conv_b.reshape(1, conv_dim), dt_bias_pad, a_log_pad,
      d_skip_ch.reshape(1, width), norm_w.reshape(1, width))


def _router_kernel(x_ref, nw_ref, wr_ref, br_ref, h_ref, meta_ref, cnt_ref, carry_ref):
    tm = x_ref.shape[0]

    @pl.when(pl.program_id(0) == 0)
    def _():
        carry_ref[...] = jnp.zeros_like(carry_ref)

    x = x_ref[...]
    h = x * lax.rsqrt(jnp.mean(x * x, axis=-1, keepdims=True) + NORM_EPS) * nw_ref[...]
    h_ref[...] = h

    wp = wr_ref[...]
    h_hi = h.astype(BF16)
    rem = h - h_hi.astype(F32)
    h_mid = rem.astype(BF16)
    h_lo = (rem - h_mid.astype(F32)).astype(BF16)
    parts = (jnp.dot(h_hi, wp, preferred_element_type=F32) + jnp.dot(h_mid, wp, preferred_element_type=F32)
             + jnp.dot(h_lo, wp, preferred_element_type=F32))
    logits = parts
    for k in range(1, 3):
        logits = logits + pltpu.roll(parts, LANES - k * ROUTER_PART_LANES, 1)
    logits = logits + br_ref[...]
    lane = lax.broadcasted_iota(I32, (tm, LANES), 1)
    neg_inf = jnp.float32(-jnp.inf)

    lane_f = lane.astype(F32)

    def first_argmax(v, vmax):
        return jnp.min(jnp.where(v == vmax, lane_f, float(LANES)), axis=1, keepdims=True).astype(I32)

    lg = jnp.where(lane < N_EXPERT_GROUPS, logits, neg_inf)
    g_max = jnp.max(lg, axis=1, keepdims=True)
    g_idx = first_argmax(lg, g_max)
    g_p = 1.0 / jnp.sum(jnp.exp(lg - g_max), axis=1, keepdims=True)

    e_lo = ROUTER_EXPERT_LANE0 + EXPERTS_PER_GROUP * g_idx
    le = jnp.where((lane >= e_lo) & (lane < e_lo + EXPERTS_PER_GROUP), logits, neg_inf)
    e_max = jnp.max(le, axis=1, keepdims=True)
    i1 = first_argmax(le, e_max)
    e_sum = jnp.sum(jnp.exp(le - e_max), axis=1, keepdims=True)
    le2 = jnp.where(lane == i1, neg_inf, le)
    e_max2 = jnp.max(le2, axis=1, keepdims=True)
    i2 = first_argmax(le2, e_max2)
    p1 = 1.0 / e_sum
    p2 = jnp.exp(e_max2 - e_max) / e_sum
    gate1 = g_p * (p1 / (p1 + p2))
    gate2 = g_p * (p2 / (p1 + p2))

    hot = ((lane == i1) | (lane == i2))
    hot_b = hot.astype(BF16)
    strict = (lax.broadcasted_iota(I32, (tm, tm), 0) > lax.broadcasted_iota(I32, (tm, tm), 1)).astype(BF16)
    before = jnp.dot(strict, hot_b, preferred_element_type=F32) + carry_ref[0:1, :]
    rank1 = jnp.sum(jnp.where(lane == i1, before, 0.0), axis=1, keepdims=True)
    rank2 = jnp.sum(jnp.where(lane == i2, before, 0.0), axis=1, keepdims=True)
    carry_ref[0:1, :] = carry_ref[0:1, :] + jnp.sum(hot.astype(F32), axis=0, keepdims=True)
    cnt_ref[...] = jnp.broadcast_to(carry_ref[0:1, :], cnt_ref.shape)

    vals = [(i1 - ROUTER_EXPERT_LANE0).astype(F32), (i2 - ROUTER_EXPERT_LANE0).astype(F32), rank1, rank2, gate1, gate2]
    meta = jnp.zeros((tm, LANES), F32)
    for c, v in enumerate(vals):
        meta = jnp.where(lane == c, v, meta)
    meta_ref[...] = meta


def _router(x1, norm_w, w_router, b_router):
    t, d = x1.shape
    return pl.pallas_call(
        _router_kernel,
        grid=(t // ROUTER_TILE,),
        in_specs=[pl.BlockSpec((ROUTER_TILE, d), lambda i: (i, 0)),
                  pl.BlockSpec((1, d), lambda i: (0, 0)),
                  pl.BlockSpec((d, LANES), lambda i: (0, 0)),
                  pl.BlockSpec((1, LANES), lambda i: (0, 0))],
        out_specs=[pl.BlockSpec((ROUTER_TILE, d), lambda i: (i, 0)),
                   pl.BlockSpec((ROUTER_TILE, LANES), lambda i: (i, 0)),
                   pl.BlockSpec((SUBLANES, LANES), lambda i: (0, 0))],
        out_shape=[jax.ShapeDtypeStruct((t, d), F32),
                   jax.ShapeDtypeStruct((t, LANES), F32),
                   jax.ShapeDtypeStruct((SUBLANES, LANES), F32)],
        scratch_shapes=[pltpu.VMEM((SUBLANES, LANES), F32)],
        compiler_params=pltpu.CompilerParams(dimension_semantics=("arbitrary",), vmem_limit_bytes=VMEM_LIMIT),
        name="router",
    )(x1, norm_w.reshape(1, d), w_router, b_router)


def _wait_rows(src_ref, dst_ref, n, sem):
    done = 0
    for rows in (64, SUBLANES, 1):
        trips = lax.shift_right_logical(n - done, rows.bit_length() - 1)

        def body(r, c, rows=rows):
            pltpu.make_async_copy(src_ref.at[pl.ds(0, rows)], dst_ref.at[pl.ds(0, rows)], sem).wait()
            return c
        lax.fori_loop(0, trips, body, 0)
        done = done + trips * rows


def _for_rows(n, body):
    n_groups = lax.shift_right_logical(n, ROW_UNROLL.bit_length() - 1)

    def group(g, c):
        base = pl.multiple_of(g * ROW_UNROLL, ROW_UNROLL)
        for u in range(ROW_UNROLL):
            body(base + u, lambda buf, u=u: buf.at[pl.ds(base, ROW_UNROLL)].at[pl.ds(u, 1)])
        return c
    lax.fori_loop(0, n_groups, group, 0)

    def tail(r, c):
        body(r, lambda buf: buf.at[pl.ds(r, 1)])
        return c
    lax.fori_loop(n_groups * ROW_UNROLL, n, tail, 0)


def _for_row_blocks(n_rows, fn):
    n_blk = lax.shift_right_logical(n_rows + (MOE_SUB - 1), MOE_SUB.bit_length() - 1)
    for blocks in range(1, MOE_ROWS // MOE_SUB + 1):
        pl.when(n_blk == blocks)(functools.partial(fn, blocks * MOE_SUB))


def _moe_up_kernel(n_items_ref, item_e_ref, item_start_ref, item_n_ref, flat_ref,
                   h_hbm, wg_ref, wu_ref, o_ref, xbuf_ref, xb_ref, sem, *, n_tokens):
    i = pl.program_id(0)
    j = pl.program_id(1)
    n_items = n_items_ref[0]
    valid = i < n_items
    conv_rows = 64

    def start_gather(item):
        start = item_start_ref[item]

        def body(r, row_of):
            tok = flat_ref[start + r] & (n_tokens - 1)
            pltpu.make_async_copy(h_hbm.at[pl.ds(tok, 1)], row_of(xbuf_ref), sem).start()
        _for_rows(item_n_ref[item], body)

    @pl.when((i == 0) & (j == 0))
    def _():
        xbuf_ref[...] = jnp.zeros_like(xbuf_ref)
        start_gather(0)

    @pl.when(valid & (j == 0))
    def _():
        _wait_rows(h_hbm, xbuf_ref, item_n_ref[i], sem)

        def convert(s, c):
            rows = pl.ds(pl.multiple_of(s * conv_rows, conv_rows), conv_rows)
            xb_ref[rows, :] = xbuf_ref[rows, :].astype(BF16)
            return c
        lax.fori_loop(0, MOE_ROWS // conv_rows, convert, 0)

        @pl.when(i + 1 < n_items)
        def _():
            start_gather(i + 1)

    o_ref[...] = jnp.zeros_like(o_ref)

    def compute(m):
        xs = xb_ref[0:m, :]
        g = jnp.dot(xs, wg_ref[...].astype(BF16), preferred_element_type=F32)
        u = jnp.dot(xs, wu_ref[...].astype(BF16), preferred_element_type=F32)
        o_ref[0:m, :] = (_silu(g) * u).astype(o_ref.dtype)

    @pl.when(valid)
    def _():
        _for_row_blocks(item_n_ref[i], compute)


def _moe_up(sched, h2, w_gate, w_up, n_items_max):
    n_tokens, _ = h2.shape
    _, d, ff = w_gate.shape
    nj = ff // MOE_FF_TILE

    def w_map(i, j, n_items, item_e, *_):
        ii = jnp.clip(i, 0, jnp.maximum(n_items[0] - 1, 0))
        return (item_e[ii], 0, jnp.where(i < n_items[0], j, nj - 1))

    def o_map(i, j, *_):
        return (i, j)

    return pl.pallas_call(
        functools.partial(_moe_up_kernel, n_tokens=n_tokens),
        grid_spec=pltpu.PrefetchScalarGridSpec(
            num_scalar_prefetch=5,
            grid=(n_items_max, nj),
            in_specs=[pl.BlockSpec(memory_space=pl.ANY),
                      pl.BlockSpec((None, d, MOE_FF_TILE), w_map),
                      pl.BlockSpec((None, d, MOE_FF_TILE), w_map)],
            out_specs=pl.BlockSpec((MOE_ROWS, MOE_FF_TILE), o_map),
            scratch_shapes=[pltpu.VMEM((MOE_ROWS, d), F32),
                            pltpu.VMEM((MOE_ROWS, d), BF16),
                            pltpu.SemaphoreType.DMA(())]),
        out_shape=jax.ShapeDtypeStruct((n_items_max * MOE_ROWS, ff), BF16),
        compiler_params=pltpu.CompilerParams(
            dimension_semantics=("arbitrary", "arbitrary"), vmem_limit_bytes=VMEM_LIMIT),
        name="moe_up",
    )(*sched, h2, w_gate, w_up)


def _moe_down_kernel(n_items_ref, item_e_ref, item_start_ref, item_n_ref, flat_ref,
                     h_ref, wd_ref, y_hbm, *scratch):
    i = pl.program_id(0)
    j = pl.program_id(1)
    valid = i < n_items_ref[0]
    tn = wd_ref.shape[1]
    ybufs = scratch[:MOE_DOWN_SPLIT]
    sems = scratch[MOE_DOWN_SPLIT:2 * MOE_DOWN_SPLIT]
    pending_ref = scratch[2 * MOE_DOWN_SPLIT]

    @pl.when((i == 0) & (j == 0))
    def _():
        for jj in range(MOE_DOWN_SPLIT):
            pending_ref[jj] = 0

    def drain(jj):
        y_part = y_hbm.at[:, pl.ds(jj * tn, tn)]
        _wait_rows(ybufs[jj], y_part, pending_ref[jj], sems[jj])
        pending_ref[jj] = 0

    for jj in range(MOE_DOWN_SPLIT):
        @pl.when(valid & (j == jj))
        def _(jj=jj):
            ybuf = ybufs[jj]
            y_part = y_hbm.at[:, pl.ds(jj * tn, tn)]
            drain(jj)
            n = item_n_ref[i]
            start = item_start_ref[i]

            def compute(m):
                ybuf[0:m, :] = jnp.dot(h_ref[0:m, :], wd_ref[...].astype(BF16), preferred_element_type=F32)
            _for_row_blocks(n, compute)

            def issue(r, row_of):
                dest = flat_ref[start + r]
                pltpu.make_async_copy(row_of(ybuf), y_part.at[pl.ds(dest, 1)], sems[jj]).start()
            _for_rows(n, issue)
            pending_ref[jj] = n

    @pl.when((i == pl.num_programs(0) - 1) & (j == MOE_DOWN_SPLIT - 1))
    def _():
        for jj in range(MOE_DOWN_SPLIT):
            drain(jj)


def _moe_down(sched, h_items, w_down, n_items_max, n_tokens):
    _, ff, d = w_down.shape
    tn = d // MOE_DOWN_SPLIT

    def w_map(i, j, n_items, item_e, *_):
        ii = jnp.clip(i, 0, jnp.maximum(n_items[0] - 1, 0))
        return (item_e[ii], 0, jnp.where(i < n_items[0], j, MOE_DOWN_SPLIT - 1))

    def h_map(i, j, n_items, *_):
        return (jnp.clip(i, 0, jnp.maximum(n_items[0] - 1, 0)), 0)

    return pl.pallas_call(
        _moe_down_kernel,
        grid_spec=pltpu.PrefetchScalarGridSpec(
            num_scalar_prefetch=5,
            grid=(n_items_max, MOE_DOWN_SPLIT),
            in_specs=[pl.BlockSpec((MOE_ROWS, ff), h_map),
                      pl.BlockSpec((None, ff, tn), w_map)],
            out_specs=pl.BlockSpec(memory_space=pl.ANY),
            scratch_shapes=([pltpu.VMEM((MOE_ROWS, tn), F32)] * MOE_DOWN_SPLIT
                            + [pltpu.SemaphoreType.DMA(())] * MOE_DOWN_SPLIT
                            + [pltpu.SMEM((MOE_DOWN_SPLIT,), I32)])),
        out_shape=jax.ShapeDtypeStruct((TOP_K * n_tokens, d), F32),
        compiler_params=pltpu.CompilerParams(
            dimension_semantics=("arbitrary", "arbitrary"), vmem_limit_bytes=VMEM_LIMIT),
        name="moe_down",
    )(*sched, h_items, w_down)


def _combine_kernel(x_ref, meta_ref, y0_ref, y1_ref, o_ref):
    o_ref[...] = x_ref[...] + meta_ref[:, 4:5] * y0_ref[...] + meta_ref[:, 5:6] * y1_ref[...]


def _combine(x1, meta, y_slots):
    t, d = x1.shape
    tm = ROUTER_TILE
    nt = t // tm
    return pl.pallas_call(
        _combine_kernel,
        grid=(nt,),
        in_specs=[pl.BlockSpec((tm, d), lambda i: (i, 0)),
                  pl.BlockSpec((tm, LANES), lambda i: (i, 0)),
                  pl.BlockSpec((tm, d), lambda i: (i, 0)),
                  pl.BlockSpec((tm, d), lambda i: (nt + i, 0))],
        out_specs=pl.BlockSpec((tm, d), lambda i: (i, 0)),
        out_shape=jax.ShapeDtypeStruct((t, d), F32),
        compiler_params=pltpu.CompilerParams(dimension_semantics=("parallel",), vmem_limit_bytes=VMEM_LIMIT),
        name="moe_combine",
    )(x1, meta, y_slots, y_slots)


def _moe_schedule(meta, counts_f, n_tokens, n_items_max):
    ids = jnp.transpose(meta[:, 0:2 * TOP_K]).astype(I32)
    e_id = ids[0:TOP_K]
    rank = ids[TOP_K:2 * TOP_K]
    counts = counts_f[0, ROUTER_EXPERT_LANE0:ROUTER_EXPERT_LANE0 + N_EXPERTS].astype(I32)
    starts = jnp.cumsum(counts) - counts
    experts = jnp.arange(N_EXPERTS, dtype=I32)[:, None, None]
    start_of = jnp.sum(jnp.where(e_id[None] == experts, starts[:, None, None], 0), axis=0)
    pos = start_of + rank
    flat = jnp.arange(TOP_K * n_tokens, dtype=I32)
    flat_sorted = jnp.zeros((TOP_K * n_tokens,), I32).at[pos.reshape(-1)].set(flat, unique_indices=True)
    chunks = (counts + MOE_ROWS - 1) // MOE_ROWS
    chunk_end = jnp.cumsum(chunks)
    n_items = chunk_end[-1]
    item = jnp.arange(n_items_max, dtype=I32)
    item_e = jnp.minimum(jnp.sum(item[:, None] >= chunk_end[None, :], axis=1), N_EXPERTS - 1).astype(I32)
    local = item - (chunk_end - chunks)[item_e]
    item_start = starts[item_e] + local * MOE_ROWS
    item_n = jnp.clip(counts[item_e] - local * MOE_ROWS, 0, MOE_ROWS)
    in_range = item < n_items
    item_start = jnp.where(in_range, item_start, 0).astype(I32)
    item_n = jnp.where(in_range, item_n, 0).astype(I32)
    return (n_items.reshape(1).astype(I32), item_e, item_start, item_n, flat_sorted)


def _layer(x, positions, norm_attn_w, w_in, q_norm_w, k_norm_w, conv_w, conv_b, dt_bias, a_log, d_skip,
           ssd_norm_w, w_out, norm_ffn_w, router_group_w, router_group_b, router_expert_w, router_expert_b,
           w_gate, w_up, w_down):
    batch, seq, d = x.shape
    t = batch * seq
    attn_width = d // 2
    n_heads = attn_width // HEAD_DIM
    ssd_width = d - attn_width
    ssd_heads = ssd_width // SSD_HEAD_DIM
    conv_dim = ssd_width + 2 * SSD_GROUPS * SSD_STATE
    main_cols = 3 * attn_width + ssd_width + conv_dim
    assert w_in.shape[1] == main_cols + ssd_heads and ssd_heads <= LANES
    assert seq % (SSD_CHUNK) == 0 and t % IN_TM == 0 and (t & (t - 1)) == 0

    x2d = x.reshape(t, d)
    h = _rmsnorm_cast(x2d, norm_attn_w)
    wt_in = jnp.transpose(w_in)
    proj = _matmul_nt(h, wt_in, main_cols, IN_TM, IN_TN, "in_proj")
    wt_dt = jnp.pad(wt_in[main_cols:, :], ((0, LANES - ssd_heads), (0, 0)))
    dt_raw = _matmul_nt(h, wt_dt, LANES, IN_TM, LANES, "dt_proj")

    half = HEAD_DIM // 2
    inv_freq = jnp.power(jnp.float32(ROPE_THETA), -jnp.arange(half, dtype=F32) / half)
    rope_tab = jnp.stack([jnp.concatenate([inv_freq, inv_freq]),
                          jnp.concatenate([-jnp.ones((half,), F32), jnp.ones((half,), F32)])])
    attn = _attention(proj, positions.reshape(t, 1), rope_tab, q_norm_w, k_norm_w, batch, seq, n_heads)

    pad_heads = lambda v: jnp.pad(v.astype(F32), (0, LANES - ssd_heads)).reshape(1, LANES)
    ssd = _ssd(proj, dt_raw, conv_w, conv_b, pad_heads(dt_bias), pad_heads(a_log),
               jnp.repeat(d_skip.astype(F32), SSD_HEAD_DIM), ssd_norm_w, batch, seq,
               z_col=(3 * attn_width) // ssd_width, xbc_col=(3 * attn_width + ssd_width) // conv_dim,
               width=ssd_width)

    x1 = _out_proj(attn, ssd, w_out, x2d, OUT_TM, OUT_TN)

    n_router = N_EXPERT_GROUPS + N_EXPERTS
    assert n_router <= ROUTER_PART_LANES and 3 * ROUTER_PART_LANES <= LANES
    w_cat = jnp.concatenate([router_group_w, router_expert_w], axis=1).astype(F32)
    w_parts, rem = [], w_cat
    for _ in range(3):
        part = rem.astype(BF16)
        w_parts.append(jnp.pad(part, ((0, 0), (0, ROUTER_PART_LANES - n_router))))
        rem = rem - part.astype(F32)
    w_router = jnp.pad(jnp.concatenate(w_parts, axis=1), ((0, 0), (0, LANES - 3 * ROUTER_PART_LANES)))
    b_router = jnp.pad(jnp.concatenate([router_group_b, router_expert_b]), (0, LANES - n_router)).reshape(1, LANES)
    h2, meta, counts_f = _router(x1, norm_ffn_w, w_router, b_router)

    n_items_max = N_EXPERTS + (TOP_K * t) // MOE_ROWS
    sched = _moe_schedule(meta, counts_f, t, n_items_max)
    h_items = _moe_up(sched, h2, w_gate, w_up, n_items_max)
    y_slots = _moe_down(sched, h_items, w_down, n_items_max, t)
    out = _combine(x1, meta, y_slots)
    return out.reshape(batch, seq, d)


def kernel(x, positions, norm_attn_w, w_in, q_norm_w, k_norm_w, conv_w, conv_b, dt_bias, a_log, d_skip, ssd_norm_w, w_out, norm_ffn_w, router_group_w, router_group_b, router_expert_w, router_expert_b, w_gate, w_up, w_down):
    for layer in range(norm_attn_w.shape[0]):
        x = _layer(x, positions, norm_attn_w[layer], w_in[layer], q_norm_w[layer], k_norm_w[layer],
                   conv_w[layer], conv_b[layer], dt_bias[layer], a_log[layer], d_skip[layer],
                   ssd_norm_w[layer], w_out[layer], norm_ffn_w[layer], router_group_w[layer],
                   router_group_b[layer], router_expert_w[layer], router_expert_b[layer],
                   w_gate[layer], w_up[layer], w_down[layer])
    return x
```

```python
import functools

import jax
import jax.numpy as jnp
from jax import lax
from jax.experimental import pallas as pl
from jax.experimental.pallas import tpu as pltpu

F32 = jnp.float32
BF16 = jnp.bfloat16
I32 = jnp.int32
HIGHEST = lax.Precision.HIGHEST

HEAD_DIM = 128
DILATED_BRANCHES = ((128, 1), (512, 4), (2048, 16))
ATTN_BLOCK = 128
ROPE_THETA = 10000.0
SSD_HEAD_DIM = 64
SSD_GROUPS = 8
SSD_STATE = 128
SSD_CONV = 4
SSD_CHUNK = 256
N_EXPERT_GROUPS = 4
EXPERTS_PER_GROUP = 8
N_EXPERTS = N_EXPERT_GROUPS * EXPERTS_PER_GROUP
TOP_K = 2
NORM_EPS = 1e-6
SSD_NORM_EPS = 1e-5

LANES = 128
SUBLANES = 8
VMEM_LIMIT = 56 * 1024 * 1024

ROW_TILE = 512
IN_TM, IN_TN = 1024, 512
OUT_TM, OUT_TN = 1024, 512
MOE_ROWS = 1024
MOE_SUB = 128
ROW_UNROLL = 8
MOE_FF_TILE = 256
MOE_DOWN_SPLIT = 2
ROUTER_TILE = 256
ROUTER_EXPERT_LANE0 = N_EXPERT_GROUPS
ROUTER_PART_LANES = 40


def _silu(v):
    return v * (1.0 / (1.0 + jnp.exp(-v)))


def _nt_dot(a, b, **kw):
    return lax.dot_general(a, b, (((1,), (1,)), ((), ())), preferred_element_type=F32, **kw)


def _rmsnorm_cast_kernel(x_ref, w_ref, o_ref):
    x = x_ref[...]
    y = x * lax.rsqrt(jnp.mean(x * x, axis=-1, keepdims=True) + NORM_EPS)
    o_ref[...] = (y * w_ref[...]).astype(o_ref.dtype)


def _rmsnorm_cast(x2d, w):
    t, d = x2d.shape
    return pl.pallas_call(
        _rmsnorm_cast_kernel,
        grid=(t // ROW_TILE,),
        in_specs=[pl.BlockSpec((ROW_TILE, d), lambda i: (i, 0)),
                  pl.BlockSpec((1, d), lambda i: (0, 0))],
        out_specs=pl.BlockSpec((ROW_TILE, d), lambda i: (i, 0)),
        out_shape=jax.ShapeDtypeStruct((t, d), BF16),
        compiler_params=pltpu.CompilerParams(dimension_semantics=("parallel",), vmem_limit_bytes=VMEM_LIMIT),
        name="rmsnorm_cast",
    )(x2d, w.reshape(1, d))


def _matmul_nt_kernel(a_ref, wt_ref, o_ref):
    o_ref[...] = _nt_dot(a_ref[...], wt_ref[...].astype(BF16)).astype(o_ref.dtype)


def _matmul_nt(a, wt, n_rows, tm, tn, name):
    m, kdim = a.shape
    return pl.pallas_call(
        _matmul_nt_kernel,
        grid=(m // tm, n_rows // tn),
        in_specs=[pl.BlockSpec((tm, kdim), lambda i, j: (i, 0)),
                  pl.BlockSpec((tn, kdim), lambda i, j: (j, 0))],
        out_specs=pl.BlockSpec((tm, tn), lambda i, j: (i, j)),
        out_shape=jax.ShapeDtypeStruct((m, n_rows), F32),
        compiler_params=pltpu.CompilerParams(
            dimension_semantics=("parallel", "parallel"), vmem_limit_bytes=VMEM_LIMIT),
        name=name,
    )(a, wt)


def _out_proj_kernel(a1_ref, a2_ref, w_ref, x_ref, o_ref):
    k1 = a1_ref.shape[1]
    acc = jnp.dot(a1_ref[...], w_ref[0:k1, :].astype(BF16), preferred_element_type=F32)
    acc = acc + jnp.dot(a2_ref[...], w_ref[k1:, :].astype(BF16), preferred_element_type=F32)
    o_ref[...] = x_ref[...] + acc


def _out_proj(a1, a2, w, x2d, tm, tn):
    m, k1 = a1.shape
    k2 = a2.shape[1]
    kdim, n = w.shape
    return pl.pallas_call(
        _out_proj_kernel,
        grid=(m // tm, n // tn),
        in_specs=[pl.BlockSpec((tm, k1), lambda i, j: (i, 0)),
                  pl.BlockSpec((tm, k2), lambda i, j: (i, 0)),
                  pl.BlockSpec((kdim, tn), lambda i, j: (0, j)),
                  pl.BlockSpec((tm, tn), lambda i, j: (i, j))],
        out_specs=pl.BlockSpec((tm, tn), lambda i, j: (i, j)),
        out_shape=jax.ShapeDtypeStruct((m, n), F32),
        compiler_params=pltpu.CompilerParams(
            dimension_semantics=("parallel", "parallel"), vmem_limit_bytes=VMEM_LIMIT),
        name="out_proj",
    )(a1, a2, w, x2d)


def _attn_kernel(pos_ref, rope_ref, qw_ref, kw_ref, q_ref, k_ref, v_ref, o_ref,
                 cos_ref, sin_ref, rms_all_ref, qf_ref, kf_ref, qd_all_ref, kd_all_ref, vd_all_ref,
                 bias_all_ref, s_all_ref, p_all_ref, m_all_ref, o_br_ref, lse_br_ref, od_all_ref, lsed_all_ref):
    seq = q_ref.shape[0]
    blk = ATTN_BLOCK
    n_blocks = seq // blk

    @pl.when(pl.program_id(1) == 0)
    def _():
        ang = pos_ref[...].astype(F32) * rope_ref[0:1, :]
        cos_ref[...] = jnp.cos(ang)
        sin_ref[...] = jnp.sin(ang) * rope_ref[1:2, :]
        for bi, (window, dil) in enumerate(DILATED_BRANCHES):
            nb = seq // dil // blk
            kw = blk if nb == 1 else 2 * blk
            k_off = blk if nb == 1 else 0
            qi = lax.broadcasted_iota(I32, (blk, kw), 0)
            kj = lax.broadcasted_iota(I32, (blk, kw), 1)
            dist = (blk - k_off) + qi - kj
            ok = (dist >= 0) & (dist <= window // dil)
            bias_all_ref[bi, 0, :, 0:kw] = jnp.where(ok, 0.0, -jnp.inf)
            bias_all_ref[bi, 1, :, 0:kw] = jnp.where(ok & (kj >= blk - k_off), 0.0, -jnp.inf)

    def norm_rot(t_ref, w_ref, dst_ref, rms_ref, scale):
        def inv_rms(i, c):
            rows = pl.ds(pl.multiple_of(i * 256, 256), 256)
            t = t_ref[rows, :]
            r = lax.rsqrt(jnp.mean(t * t, axis=-1, keepdims=True) + NORM_EPS)
            rms_ref[rows, :] = jnp.broadcast_to(r, (256, HEAD_DIM))
            return c
        lax.fori_loop(0, seq // 256, inv_rms, 0, unroll=True)

        def rotate(i, c):
            rows = pl.ds(pl.multiple_of(i * 256, 256), 256)
            y = t_ref[rows, :] * rms_ref[rows, :] * w_ref[...]
            y = y * cos_ref[rows, :] + pltpu.roll(y, HEAD_DIM // 2, 1) * sin_ref[rows, :]
            dst_ref[rows, :] = y * scale if scale != 1.0 else y
            return c
        lax.fori_loop(0, seq // 256, rotate, 0, unroll=True)

    norm_rot(q_ref, qw_ref, qf_ref, rms_all_ref.at[0], HEAD_DIM ** -0.5)
    norm_rot(k_ref, kw_ref, kf_ref, rms_all_ref.at[1], 1.0)

    for bi, (window, dil) in enumerate(DILATED_BRANCHES):
        qd_ref, kd_ref, vd_ref = qd_all_ref.at[bi], kd_all_ref.at[bi], vd_all_ref.at[bi]
        bias_ref, s_ref, p_ref, m_ref = bias_all_ref.at[bi], s_all_ref.at[bi], p_all_ref.at[bi], m_all_ref.at[bi]
        od_ref, lsed_ref = od_all_ref.at[bi], lsed_all_ref.at[bi]
        kd_ref[0:blk, :] = jnp.zeros((blk, HEAD_DIM), BF16)
        vd_ref[0:blk, 0:HEAD_DIM] = jnp.zeros((blk, HEAD_DIM), BF16)
        vd_ref[:, HEAD_DIM:] = jnp.ones((blk + seq, HEAD_DIM), BF16)
        n_back = window // dil
        sub_len = seq // dil
        nb = sub_len // blk
        natural = dil == 1
        kw = blk if nb == 1 else 2 * blk
        k_off = blk if nb == 1 else 0
        for r in range(dil):
            src = slice(None) if natural else pl.ds(r, sub_len, stride=dil)
            dst = slice(r * sub_len, (r + 1) * sub_len)
            dstp = slice(blk + r * sub_len, blk + (r + 1) * sub_len)
            qd_ref[dst, :] = qf_ref[src, :].astype(BF16)
            kd_ref[dstp, :] = kf_ref[src, :].astype(BF16)
            vd_ref[dstp, 0:HEAD_DIM] = v_ref[src, :].astype(BF16)

        o_out = o_br_ref.at[bi] if natural else od_ref
        lse_out = lse_br_ref.at[bi] if natural else lsed_ref

        def scores(j, c):
            r0 = pl.multiple_of(j * blk, blk)
            s_ref[pl.ds(r0, blk), 0:kw] = _nt_dot(qd_ref[pl.ds(r0, blk), :], kd_ref[pl.ds(r0 + k_off, kw), :])
            return c
        lax.fori_loop(0, n_blocks, scores, 0, unroll=True)

        def softmax(j, c):
            r0 = pl.multiple_of(j * blk, blk)
            first = jnp.where((j & (nb - 1)) == 0, 1, 0)
            s = s_ref[pl.ds(r0, blk), 0:kw] + bias_ref[first, :, 0:kw]
            m = jnp.max(s, axis=1, keepdims=True)
            p_ref[pl.ds(r0, blk), 0:kw] = jnp.exp(s - m).astype(BF16)
            m_ref[pl.ds(r0, blk), :] = jnp.broadcast_to(m, (blk, HEAD_DIM))
            return c
        lax.fori_loop(0, n_blocks, softmax, 0, unroll=True)

        def weighted(j, c):
            r0 = pl.multiple_of(j * blk, blk)
            acc = jnp.dot(p_ref[pl.ds(r0, blk), 0:kw], vd_ref[pl.ds(r0 + k_off, kw), :], preferred_element_type=F32)
            l = acc[:, HEAD_DIM:]
            o_out[pl.ds(r0, blk), :] = acc[:, 0:HEAD_DIM] / l
            lse_out[pl.ds(r0, blk), :] = m_ref[pl.ds(r0, blk), :] + jnp.log(l)
            return c
        lax.fori_loop(0, n_blocks, weighted, 0, unroll=True)

        if not natural:
            for r in range(dil):
                src = slice(r * sub_len, (r + 1) * sub_len)
                dst = pl.ds(r, sub_len, stride=dil)
                o_br_ref[bi, dst, :] = od_ref[src, :]
                lse_br_ref[bi, dst, :] = lsed_ref[src, :]

    n_br = len(DILATED_BRANCHES)

    def merge(i, c):
        rows = pl.ds(pl.multiple_of(i * 256, 256), 256)
        lses = [lse_br_ref[b, rows, :] for b in range(n_br)]
        top = functools.reduce(jnp.maximum, lses)
        num = jnp.zeros((256, HEAD_DIM), F32)
        den = jnp.zeros((256, HEAD_DIM), F32)
        for b in range(n_br):
            e = jnp.exp(lses[b] - top)
            num = num + e * o_br_ref[b, rows, :]
            den = den + e
        o_ref[rows, :] = (num / den).astype(o_ref.dtype)
        return c
    lax.fori_loop(0, seq // 256, merge, 0, unroll=True)


def _attention(proj, pos_col, rope_tab, q_norm_w, k_norm_w, batch, seq, n_heads):
    t = batch * seq
    nbr = len(DILATED_BRANCHES)
    qkv_spec = lambda off: pl.BlockSpec((seq, HEAD_DIM), lambda b, h: (b, off + h))
    return pl.pallas_call(
        _attn_kernel,
        grid=(batch, n_heads),
        in_specs=[pl.BlockSpec((seq, 1), lambda b, h: (b, 0)),
                  pl.BlockSpec((2, HEAD_DIM), lambda b, h: (0, 0)),
                  pl.BlockSpec((1, HEAD_DIM), lambda b, h: (0, 0)),
                  pl.BlockSpec((1, HEAD_DIM), lambda b, h: (0, 0)),
                  qkv_spec(0), qkv_spec(n_heads), qkv_spec(2 * n_heads)],
        out_specs=pl.BlockSpec((seq, HEAD_DIM), lambda b, h: (b, h)),
        out_shape=jax.ShapeDtypeStruct((t, n_heads * HEAD_DIM), BF16),
        scratch_shapes=[
            pltpu.VMEM((seq, HEAD_DIM), F32),
            pltpu.VMEM((seq, HEAD_DIM), F32),
            pltpu.VMEM((2, seq, HEAD_DIM), F32),
            pltpu.VMEM((seq, HEAD_DIM), F32),
            pltpu.VMEM((seq, HEAD_DIM), F32),
            pltpu.VMEM((nbr, seq, HEAD_DIM), BF16),
            pltpu.VMEM((nbr, ATTN_BLOCK + seq, HEAD_DIM), BF16),
            pltpu.VMEM((nbr, ATTN_BLOCK + seq, 2 * HEAD_DIM), BF16),
            pltpu.VMEM((nbr, 2, ATTN_BLOCK, 2 * ATTN_BLOCK), F32),
            pltpu.VMEM((nbr, seq, 2 * ATTN_BLOCK), F32),
            pltpu.VMEM((nbr, seq, 2 * ATTN_BLOCK), BF16),
            pltpu.VMEM((nbr, seq, HEAD_DIM), F32),
            pltpu.VMEM((nbr, seq, HEAD_DIM), F32),
            pltpu.VMEM((nbr, seq, HEAD_DIM), F32),
            pltpu.VMEM((nbr, seq, HEAD_DIM), F32),
            pltpu.VMEM((nbr, seq, HEAD_DIM), F32),
        ],
        compiler_params=pltpu.CompilerParams(
            dimension_semantics=("parallel", "arbitrary"), vmem_limit_bytes=VMEM_LIMIT),
        name="dilated_attention",
    )(pos_col, rope_tab, q_norm_w.reshape(1, HEAD_DIM), k_norm_w.reshape(1, HEAD_DIM), proj, proj, proj)


def _ssd_kernel(z_ref, xbc_ref, dt_ref, convw_ref, convb_ref, dtb_ref, alog_ref, dskip_ref, normw_ref,
                o_ref, ext_ref, act_ref, state_ref, cs_ref, ecs_ref, cst_ref, wt_ref, dtt_ref, cdec_ref):
    lc = SSD_CHUNK
    width = o_ref.shape[1]
    gw = width // SSD_GROUPS
    heads_per_group = gw // SSD_HEAD_DIM
    conv_dim = xbc_ref.shape[1]
    halo = SUBLANES

    @pl.when(pl.program_id(1) == 0)
    def _():
        ext_ref[0:halo, :] = jnp.zeros((halo, conv_dim), F32)
        state_ref[...] = jnp.zeros_like(state_ref)

    ext_ref[halo:halo + lc, :] = xbc_ref[...]
    col_tile = 512
    for ct in range(conv_dim // col_tile):
        cols = slice(ct * col_tile, (ct + 1) * col_tile)
        acc = jnp.broadcast_to(convb_ref[:, cols], (lc, col_tile))
        for j in range(SSD_CONV):
            r0 = halo - (SSD_CONV - 1) + j
            acc = acc + convw_ref[j:j + 1, cols] * ext_ref[r0:r0 + lc, cols]
        act_ref[:, cols] = _silu(acc)
    ext_ref[0:halo, :] = xbc_ref[lc - halo:lc, :]

    x0 = dt_ref[...] + dtb_ref[...]
    dt = jnp.maximum(x0, 0.0) + jnp.log1p(jnp.exp(-jnp.abs(x0)))
    d_a = dt * (-jnp.exp(alog_ref[...]))
    tri = (lax.broadcasted_iota(I32, (lc, lc), 0) >= lax.broadcasted_iota(I32, (lc, lc), 1)).astype(F32)
    cs = jnp.dot(tri, d_a, precision=HIGHEST, preferred_element_type=F32)
    cs_last = cs[lc - 1:lc, :]
    cs_ref[...] = cs
    ecs_ref[...] = jnp.exp(cs)
    cdec_ref[...] = jnp.exp(cs_last)
    eye = (lax.broadcasted_iota(I32, (LANES, LANES), 0) == lax.broadcasted_iota(I32, (LANES, LANES), 1)).astype(F32)
    cst_ref[...] = _nt_dot(eye, cs, precision=HIGHEST)
    wt_ref[...] = _nt_dot(eye, jnp.exp(cs_last - cs) * dt, precision=HIGHEST)
    dtt_ref[...] = _nt_dot(eye, dt, precision=HIGHEST)

    causal = lax.broadcasted_iota(I32, (lc, lc), 0) >= lax.broadcasted_iota(I32, (lc, lc), 1)
    lane_head = lax.broadcasted_iota(I32, (lc, gw), 1) // SSD_HEAD_DIM
    lane_head_s = lax.broadcasted_iota(I32, (SSD_STATE, gw), 1) // SSD_HEAD_DIM
    n_xs = width
    n_b = SSD_GROUPS * SSD_STATE

    for g in range(SSD_GROUPS):
        gc = slice(g * gw, (g + 1) * gw)
        xs = act_ref[:, gc]
        b_g = act_ref[:, n_xs + g * SSD_STATE:n_xs + (g + 1) * SSD_STATE]
        c_g = act_ref[:, n_xs + n_b + g * SSD_STATE:n_xs + n_b + (g + 1) * SSD_STATE]
        xs_b = xs.astype(BF16)
        c_b = c_g.astype(BF16)
        cb = _nt_dot(c_b, b_g.astype(BF16))
        b_t = jnp.transpose(b_g)
        s_prev = state_ref[g]
        y_off = jnp.dot(c_b, s_prev.astype(BF16), preferred_element_type=F32)
        y = jnp.zeros((lc, gw), F32)
        s_new = jnp.zeros((SSD_STATE, gw), F32)
        for r in range(heads_per_group):
            h = g * heads_per_group + r
            diff = cs_ref[:, h:h + 1] - cst_ref[h:h + 1, :]
            seg = jnp.exp(jnp.where(causal, diff, -jnp.inf))
            m_h = (cb * seg * dtt_ref[h:h + 1, :]).astype(BF16)
            y_h = jnp.dot(m_h, xs_b, preferred_element_type=F32)
            bw = (b_t * wt_ref[h:h + 1, :]).astype(BF16)
            s_h = jnp.dot(bw, xs_b, preferred_element_type=F32)
            y = jnp.where(lane_head == r, y_h + ecs_ref[:, h:h + 1] * y_off, y)
            s_new = jnp.where(lane_head_s == r, cdec_ref[:, h:h + 1] * s_prev + s_h, s_new)
        state_ref[g] = s_new
        y = y + dskip_ref[:, gc] * xs
        y = y * _silu(z_ref[:, gc])
        y = y * lax.rsqrt(jnp.mean(y * y, axis=-1, keepdims=True) + SSD_NORM_EPS)
        o_ref[:, gc] = (y * normw_ref[:, gc]).astype(o_ref.dtype)


def _ssd(proj, dt_raw, conv_w, conv_b, dt_bias_pad, a_log_pad, d_skip_ch, norm_w, batch, seq, z_col, xbc_col, width):
    t = batch * seq
    nc = seq // SSD_CHUNK
    conv_dim = conv_w.shape[1]
    gw = width // SSD_GROUPS
    row = lambda b, c: b * nc + c
    return pl.pallas_call(
        _ssd_kernel,
        grid=(batch, nc),
        in_specs=[pl.BlockSpec((SSD_CHUNK, width), lambda b, c: (row(b, c), z_col)),
                  pl.BlockSpec((SSD_CHUNK, conv_dim), lambda b, c: (row(b, c), xbc_col)),
                  pl.BlockSpec((SSD_CHUNK, LANES), lambda b, c: (row(b, c), 0)),
                  pl.BlockSpec((SSD_CONV, conv_dim), lambda b, c: (0, 0)),
                  pl.BlockSpec((1, conv_dim), lambda b, c: (0, 0)),
                  pl.BlockSpec((1, LANES), lambda b, c: (0, 0)),
                  pl.BlockSpec((1, LANES), lambda b, c: (0, 0)),
                  pl.BlockSpec((1, width), lambda b, c: (0, 0)),
                  pl.BlockSpec((1, width), lambda b, c: (0, 0))],
        out_specs=pl.BlockSpec((SSD_CHUNK, width), lambda b, c: (row(b, c), 0)),
        out_shape=jax.ShapeDtypeStruct((t, width), BF16),
        scratch_shapes=[
            pltpu.VMEM((SUBLANES + SSD_CHUNK, conv_dim), F32),
            pltpu.VMEM((SSD_CHUNK, conv_dim), F32),
            pltpu.VMEM((SSD_GROUPS, SSD_STATE, gw), F32),
            pltpu.VMEM((SSD_CHUNK, LANES), F32),
            pltpu.VMEM((SSD_CHUNK, LANES), F32),
            pltpu.VMEM((LANES, SSD_CHUNK), F32),
            pltpu.VMEM((LANES, SSD_CHUNK), F32),
            pltpu.VMEM((LANES, SSD_CHUNK), F32),
            pltpu.VMEM((1, LANES), F32),
        ],
        compiler_params=pltpu.CompilerParams(
            dimension_semantics=("parallel", "arbitrary"), vmem_limit_bytes=VMEM_LIMIT),
        name="ssd_mixer",
    )(proj, proj, dt_raw, conv_w, conv_b.reshape(1, conv_dim), dt_bias_pad, a_log_pad,
      d_skip_ch.reshape(1, width), norm_w.reshape(1, width))


def _router_kernel(x_ref, nw_ref, wr_ref, br_ref, h_ref, meta_ref, cnt_ref, carry_ref):
    tm = x_ref.shape[0]

    @pl.when(pl.program_id(0) == 0)
    def _():
        carry_ref[...] = jnp.zeros_like(carry_ref)

    x = x_ref[...]
    h = x * lax.rsqrt(jnp.mean(x * x, axis=-1, keepdims=True) + NORM_EPS) * nw_ref[...]
    h_ref[...] = h

    wp = wr_ref[...]
    h_hi = h.astype(BF16)
    rem = h - h_hi.astype(F32)
    h_mid = rem.astype(BF16)
    h_lo = (rem - h_mid.astype(F32)).astype(BF16)
    parts = (jnp.dot(h_hi, wp, preferred_element_type=F32) + jnp.dot(h_mid, wp, preferred_element_type=F32)
             + jnp.dot(h_lo, wp, preferred_element_type=F32))
    logits = parts
    for k in range(1, 3):
        logits = logits + pltpu.roll(parts, LANES - k * ROUTER_PART_LANES, 1)
    logits = logits + br_ref[...]
    lane = lax.broadcasted_iota(I32, (tm, LANES), 1)
    neg_inf = jnp.float32(-jnp.inf)

    lane_f = lane.astype(F32)

    def first_argmax(v, vmax):
        return jnp.min(jnp.where(v == vmax, lane_f, float(LANES)), axis=1, keepdims=True).astype(I32)

    lg = jnp.where(lane < N_EXPERT_GROUPS, logits, neg_inf)
    g_max = jnp.max(lg, axis=1, keepdims=True)
    g_idx = first_argmax(lg, g_max)
    g_p = 1.0 / jnp.sum(jnp.exp(lg - g_max), axis=1, keepdims=True)

    e_lo = ROUTER_EXPERT_LANE0 + EXPERTS_PER_GROUP * g_idx
    le = jnp.where((lane >= e_lo) & (lane < e_lo + EXPERTS_PER_GROUP), logits, neg_inf)
    e_max = jnp.max(le, axis=1, keepdims=True)
    i1 = first_argmax(le, e_max)
    e_sum = jnp.sum(jnp.exp(le - e_max), axis=1, keepdims=True)
    le2 = jnp.where(lane == i1, neg_inf, le)
    e_max2 = jnp.max(le2, axis=1, keepdims=True)
    i2 = first_argmax(le2, e_max2)
    p1 = 1.0 / e_sum
    p2 = jnp.exp(e_max2 - e_max) / e_sum
    gate1 = g_p * (p1 / (p1 + p2))
    gate2 = g_p * (p2 / (p1 + p2))

    hot = ((lane == i1) | (lane == i2))
    hot_b = hot.astype(BF16)
    strict = (lax.broadcasted_iota(I32, (tm, tm), 0) > lax.broadcasted_iota(I32, (tm, tm), 1)).astype(BF16)
    before = jnp.dot(strict, hot_b, preferred_element_type=F32) + carry_ref[0:1, :]
    rank1 = jnp.sum(jnp.where(lane == i1, before, 0.0), axis=1, keepdims=True)
    rank2 = jnp.sum(jnp.where(lane == i2, before, 0.0), axis=1, keepdims=True)
    carry_ref[0:1, :] = carry_ref[0:1, :] + jnp.sum(hot.astype(F32), axis=0, keepdims=True)
    cnt_ref[...] = jnp.broadcast_to(carry_ref[0:1, :], cnt_ref.shape)

    vals = [(i1 - ROUTER_EXPERT_LANE0).astype(F32), (i2 - ROUTER_EXPERT_LANE0).astype(F32), rank1, rank2, gate1, gate2]
    meta = jnp.zeros((tm, LANES), F32)
    for c, v in enumerate(vals):
        meta = jnp.where(lane == c, v, meta)
    meta_ref[...] = meta


def _router(x1, norm_w, w_router, b_router):
    t, d = x1.shape
    return pl.pallas_call(
        _router_kernel,
        grid=(t // ROUTER_TILE,),
        in_specs=[pl.BlockSpec((ROUTER_TILE, d), lambda i: (i, 0)),
                  pl.BlockSpec((1, d), lambda i: (0, 0)),
                  pl.BlockSpec((d, LANES), lambda i: (0, 0)),
                  pl.BlockSpec((1, LANES), lambda i: (0, 0))],
        out_specs=[pl.BlockSpec((ROUTER_TILE, d), lambda i: (i, 0)),
                   pl.BlockSpec((ROUTER_TILE, LANES), lambda i: (i, 0)),
                   pl.BlockSpec((SUBLANES, LANES), lambda i: (0, 0))],
        out_shape=[jax.ShapeDtypeStruct((t, d), F32),
                   jax.ShapeDtypeStruct((t, LANES), F32),
                   jax.ShapeDtypeStruct((SUBLANES, LANES), F32)],
        scratch_shapes=[pltpu.VMEM((SUBLANES, LANES), F32)],
        compiler_params=pltpu.CompilerParams(dimension_semantics=("arbitrary",), vmem_limit_bytes=VMEM_LIMIT),
        name="router",
    )(x1, norm_w.reshape(1, d), w_router, b_router)


def _wait_rows(src_ref, dst_ref, n, sem):
    done = 0
    for rows in (64, SUBLANES, 1):
        trips = lax.shift_right_logical(n - done, rows.bit_length() - 1)

        def body(r, c, rows=rows):
            pltpu.make_async_copy(src_ref.at[pl.ds(0, rows)], dst_ref.at[pl.ds(0, rows)], sem).wait()
            return c
        lax.fori_loop(0, trips, body, 0)
        done = done + trips * rows


def _for_rows(n, body):
    n_groups = lax.shift_right_logical(n, ROW_UNROLL.bit_length() - 1)

    def group(g, c):
        base = pl.multiple_of(g * ROW_UNROLL, ROW_UNROLL)
        for u in range(ROW_UNROLL):
            body(base + u, lambda buf, u=u: buf.at[pl.ds(base, ROW_UNROLL)].at[pl.ds(u, 1)])
        return c
    lax.fori_loop(0, n_groups, group, 0)

    def tail(r, c):
        body(r, lambda buf: buf.at[pl.ds(r, 1)])
        return c
    lax.fori_loop(n_groups * ROW_UNROLL, n, tail, 0)


def _for_row_blocks(n_rows, fn):
    n_blk = lax.shift_right_logical(n_rows + (MOE_SUB - 1), MOE_SUB.bit_length() - 1)
    for blocks in range(1, MOE_ROWS // MOE_SUB + 1):
        pl.when(n_blk == blocks)(functools.partial(fn, blocks * MOE_SUB))


def _moe_up_kernel(n_items_ref, item_e_ref, item_start_ref, item_n_ref, flat_ref,
                   h_hbm, wg_ref, wu_ref, o_ref, xbuf_ref, xb_ref, sem, *, n_tokens):
    i = pl.program_id(0)
    j = pl.program_id(1)
    n_items = n_items_ref[0]
    valid = i < n_items
    conv_rows = 64

    def start_gather(item):
        start = item_start_ref[item]

        def body(r, row_of):
            tok = flat_ref[start + r] & (n_tokens - 1)
            pltpu.make_async_copy(h_hbm.at[pl.ds(tok, 1)], row_of(xbuf_ref), sem).start()
        _for_rows(item_n_ref[item], body)

    @pl.when((i == 0) & (j == 0))
    def _():
        xbuf_ref[...] = jnp.zeros_like(xbuf_ref)
        start_gather(0)

    @pl.when(valid & (j == 0))
    def _():
        _wait_rows(h_hbm, xbuf_ref, item_n_ref[i], sem)

        def convert(s, c):
            rows = pl.ds(pl.multiple_of(s * conv_rows, conv_rows), conv_rows)
            xb_ref[rows, :] = xbuf_ref[rows, :].astype(BF16)
            return c
        lax.fori_loop(0, MOE_ROWS // conv_rows, convert, 0)

        @pl.when(i + 1 < n_items)
        def _():
            start_gather(i + 1)

    o_ref[...] = jnp.zeros_like(o_ref)

    def compute(m):
        xs = xb_ref[0:m, :]
        g = jnp.dot(xs, wg_ref[...].astype(BF16), preferred_element_type=F32)
        u = jnp.dot(xs, wu_ref[...].astype(BF16), preferred_element_type=F32)
        o_ref[0:m, :] = (_silu(g) * u).astype(o_ref.dtype)

    @pl.when(valid)
    def _():
        _for_row_blocks(item_n_ref[i], compute)


def _moe_up(sched, h2, w_gate, w_up, n_items_max):
    n_tokens, _ = h2.shape
    _, d, ff = w_gate.shape
    nj = ff // MOE_FF_TILE

    def w_map(i, j, n_items, item_e, *_):
        ii = jnp.clip(i, 0, jnp.maximum(n_items[0] - 1, 0))
        return (item_e[ii], 0, jnp.where(i < n_items[0], j, nj - 1))

    def o_map(i, j, *_):
        return (i, j)

    return pl.pallas_call(
        functools.partial(_moe_up_kernel, n_tokens=n_tokens),
        grid_spec=pltpu.PrefetchScalarGridSpec(
            num_scalar_prefetch=5,
            grid=(n_items_max, nj),
            in_specs=[pl.BlockSpec(memory_space=pl.ANY),
                      pl.BlockSpec((None, d, MOE_FF_TILE), w_map),
                      pl.BlockSpec((None, d, MOE_FF_TILE), w_map)],
            out_specs=pl.BlockSpec((MOE_ROWS, MOE_FF_TILE), o_map),
            scratch_shapes=[pltpu.VMEM((MOE_ROWS, d), F32),
                            pltpu.VMEM((MOE_ROWS, d), BF16),
                            pltpu.SemaphoreType.DMA(())]),
        out_shape=jax.ShapeDtypeStruct((n_items_max * MOE_ROWS, ff), BF16),
        compiler_params=pltpu.CompilerParams(
            dimension_semantics=("arbitrary", "arbitrary"), vmem_limit_bytes=VMEM_LIMIT),
        name="moe_up",
    )(*sched, h2, w_gate, w_up)


def _moe_down_kernel(n_items_ref, item_e_ref, item_start_ref, item_n_ref, flat_ref,
                     h_ref, wd_ref, y_hbm, *scratch):
    i = pl.program_id(0)
    j = pl.program_id(1)
    valid = i < n_items_ref[0]
    tn = wd_ref.shape[1]
    ybufs = scratch[:MOE_DOWN_SPLIT]
    sems = scratch[MOE_DOWN_SPLIT:2 * MOE_DOWN_SPLIT]
    pending_ref = scratch[2 * MOE_DOWN_SPLIT]

    @pl.when((i == 0) & (j == 0))
    def _():
        for jj in range(MOE_DOWN_SPLIT):
            pending_ref[jj] = 0

    def drain(jj):
        y_part = y_hbm.at[:, pl.ds(jj * tn, tn)]
        _wait_rows(ybufs[jj], y_part, pending_ref[jj], sems[jj])
        pending_ref[jj] = 0

    for jj in range(MOE_DOWN_SPLIT):
        @pl.when(valid & (j == jj))
        def _(jj=jj):
            ybuf = ybufs[jj]
            y_part = y_hbm.at[:, pl.ds(jj * tn, tn)]
            drain(jj)
            n = item_n_ref[i]
            start = item_start_ref[i]

            def compute(m):
                ybuf[0:m, :] = jnp.dot(h_ref[0:m, :], wd_ref[...].astype(BF16), preferred_element_type=F32)
            _for_row_blocks(n, compute)

            def issue(r, row_of):
                dest = flat_ref[start + r]
                pltpu.make_async_copy(row_of(ybuf), y_part.at[pl.ds(dest, 1)], sems[jj]).start()
            _for_rows(n, issue)
            pending_ref[jj] = n

    @pl.when((i == pl.num_programs(0) - 1) & (j == MOE_DOWN_SPLIT - 1))
    def _():
        for jj in range(MOE_DOWN_SPLIT):
            drain(jj)


def _moe_down(sched, h_items, w_down, n_items_max, n_tokens):
    _, ff, d = w_down.shape
    tn = d // MOE_DOWN_SPLIT

    def w_map(i, j, n_items, item_e, *_):
        ii = jnp.clip(i, 0, jnp.maximum(n_items[0] - 1, 0))
        return (item_e[ii], 0, jnp.where(i < n_items[0], j, MOE_DOWN_SPLIT - 1))

    def h_map(i, j, n_items, *_):
        return (jnp.clip(i, 0, jnp.maximum(n_items[0] - 1, 0)), 0)

    return pl.pallas_call(
        _moe_down_kernel,
        grid_spec=pltpu.PrefetchScalarGridSpec(
            num_scalar_prefetch=5,
            grid=(n_items_max, MOE_DOWN_SPLIT),
            in_specs=[pl.BlockSpec((MOE_ROWS, ff), h_map),
                      pl.BlockSpec((None, ff, tn), w_map)],
            out_specs=pl.BlockSpec(memory_space=pl.ANY),
            scratch_shapes=([pltpu.VMEM((MOE_ROWS, tn), F32)] * MOE_DOWN_SPLIT
                            + [pltpu.SemaphoreType.DMA(())] * MOE_DOWN_SPLIT
                            + [pltpu.SMEM((MOE_DOWN_SPLIT,), I32)])),
        out_shape=jax.ShapeDtypeStruct((TOP_K * n_tokens, d), F32),
        compiler_params=pltpu.CompilerParams(
            dimension_semantics=("arbitrary", "arbitrary"), vmem_limit_bytes=VMEM_LIMIT),
        name="moe_down",
    )(*sched, h_items, w_down)


def _combine_kernel(x_ref, meta_ref, y0_ref, y1_ref, o_ref):
    o_ref[...] = x_ref[...] + meta_ref[:, 4:5] * y0_ref[...] + meta_ref[:, 5:6] * y1_ref[...]


def _combine(x1, meta, y_slots):
    t, d = x1.shape
    tm = ROUTER_TILE
    nt = t // tm
    return pl.pallas_call(
        _combine_kernel,
        grid=(nt,),
        in_specs=[pl.BlockSpec((tm, d), lambda i: (i, 0)),
                  pl.BlockSpec((tm, LANES), lambda i: (i, 0)),
                  pl.BlockSpec((tm, d), lambda i: (i, 0)),
                  pl.BlockSpec((tm, d), lambda i: (nt + i, 0))],
        out_specs=pl.BlockSpec((tm, d), lambda i: (i, 0)),
        out_shape=jax.ShapeDtypeStruct((t, d), F32),
        compiler_params=pltpu.CompilerParams(dimension_semantics=("parallel",), vmem_limit_bytes=VMEM_LIMIT),
        name="moe_combine",
    )(x1, meta, y_slots, y_slots)


def _invert_permutation_kernel(pos_ref, inv_ref):
    def body(g, c):
        for u in range(ROW_UNROLL):
            a = g * ROW_UNROLL + u
            inv_ref[pos_ref[a]] = a
        return c
    lax.fori_loop(0, pos_ref.shape[0] // ROW_UNROLL, body, 0)


def _invert_permutation(pos):
    return pl.pallas_call(
        _invert_permutation_kernel,
        in_specs=[pl.BlockSpec(memory_space=pltpu.SMEM)],
        out_specs=pl.BlockSpec(memory_space=pltpu.SMEM),
        out_shape=jax.ShapeDtypeStruct(pos.shape, I32),
        name="invert_permutation",
    )(pos)


def _moe_schedule(meta, counts_f, n_tokens, n_items_max):
    ids = jnp.transpose(meta[:, 0:2 * TOP_K]).astype(I32)
    e_id = ids[0:TOP_K]
    rank = ids[TOP_K:2 * TOP_K]
    counts = counts_f[0, ROUTER_EXPERT_LANE0:ROUTER_EXPERT_LANE0 + N_EXPERTS].astype(I32)
    starts = jnp.cumsum(counts) - counts
    experts = jnp.arange(N_EXPERTS, dtype=I32)[:, None, None]
    start_of = jnp.sum(jnp.where(e_id[None] == experts, starts[:, None, None], 0), axis=0)
    pos = start_of + rank
    flat_sorted = _invert_permutation(pos.reshape(-1))
    chunks = (counts + MOE_ROWS - 1) // MOE_ROWS
    chunk_end = jnp.cumsum(chunks)
    n_items = chunk_end[-1]
    item = jnp.arange(n_items_max, dtype=I32)
    item_e = jnp.minimum(jnp.sum(item[:, None] >= chunk_end[None, :], axis=1), N_EXPERTS - 1).astype(I32)
    local = item - (chunk_end - chunks)[item_e]
    item_start = starts[item_e] + local * MOE_ROWS
    item_n = jnp.clip(counts[item_e] - local * MOE_ROWS, 0, MOE_ROWS)
    in_range = item < n_items
    item_start = jnp.where(in_range, item_start, 0).astype(I32)
    item_n = jnp.where(in_range, item_n, 0).astype(I32)
    return (n_items.reshape(1).astype(I32), item_e, item_start, item_n, flat_sorted)


def _layer(x, positions, norm_attn_w, w_in, q_norm_w, k_norm_w, conv_w, conv_b, dt_bias, a_log, d_skip,
           ssd_norm_w, w_out, norm_ffn_w, router_group_w, router_group_b, router_expert_w, router_expert_b,
           w_gate, w_up, w_down):
    batch, seq, d = x.shape
    t = batch * seq
    attn_width = d // 2
    n_heads = attn_width // HEAD_DIM
    ssd_width = d - attn_width
    ssd_heads = ssd_width // SSD_HEAD_DIM
    conv_dim = ssd_width + 2 * SSD_GROUPS * SSD_STATE
    main_cols = 3 * attn_width + ssd_width + conv_dim
    assert w_in.shape[1] == main_cols + ssd_heads and ssd_heads <= LANES
    assert seq % (SSD_CHUNK) == 0 and t % IN_TM == 0 and (t & (t - 1)) == 0

    x2d = x.reshape(t, d)
    h = _rmsnorm_cast(x2d, norm_attn_w)
    wt_in = jnp.transpose(w_in)
    proj = _matmul_nt(h, wt_in, main_cols, IN_TM, IN_TN, "in_proj")
    wt_dt = jnp.pad(wt_in[main_cols:, :], ((0, LANES - ssd_heads), (0, 0)))
    dt_raw = _matmul_nt(h, wt_dt, LANES, IN_TM, LANES, "dt_proj")

    half = HEAD_DIM // 2
    inv_freq = jnp.power(jnp.float32(ROPE_THETA), -jnp.arange(half, dtype=F32) / half)
    rope_tab = jnp.stack([jnp.concatenate([inv_freq, inv_freq]),
                          jnp.concatenate([-jnp.ones((half,), F32), jnp.ones((half,), F32)])])
    attn = _attention(proj, positions.reshape(t, 1), rope_tab, q_norm_w, k_norm_w, batch, seq, n_heads)

    pad_heads = lambda v: jnp.pad(v.astype(F32), (0, LANES - ssd_heads)).reshape(1, LANES)
    ssd = _ssd(proj, dt_raw, conv_w, conv_b, pad_heads(dt_bias), pad_heads(a_log),
               jnp.repeat(d_skip.astype(F32), SSD_HEAD_DIM), ssd_norm_w, batch, seq,
               z_col=(3 * attn_width) // ssd_width, xbc_col=(3 * attn_width + ssd_width) // conv_dim,
               width=ssd_width)

    x1 = _out_proj(attn, ssd, w_out, x2d, OUT_TM, OUT_TN)

    n_router = N_EXPERT_GROUPS + N_EXPERTS
    assert n_router <= ROUTER_PART_LANES and 3 * ROUTER_PART_LANES <= LANES
    w_cat = jnp.concatenate([router_group_w, router_expert_w], axis=1).astype(F32)
    w_parts, rem = [], w_cat
    for _ in range(3):
        part = rem.astype(BF16)
        w_parts.append(jnp.pad(part, ((0, 0), (0, ROUTER_PART_LANES - n_router))))
        rem = rem - part.astype(F32)
    w_router = jnp.pad(jnp.concatenate(w_parts, axis=1), ((0, 0), (0, LANES - 3 * ROUTER_PART_LANES)))
    b_router = jnp.pad(jnp.concatenate([router_group_b, router_expert_b]), (0, LANES - n_router)).reshape(1, LANES)
    h2, meta, counts_f = _router(x1, norm_ffn_w, w_router, b_router)

    n_items_max = N_EXPERTS + (TOP_K * t) // MOE_ROWS
    sched = _moe_schedule(meta, counts_f, t, n_items_max)
    h_items = _moe_up(sched, h2, w_gate, w_up, n_items_max)
    y_slots = _moe_down(sched, h_items, w_down, n_items_max, t)
    out = _combine(x1, meta, y_slots)
    return out.reshape(batch, seq, d)


def kernel(x, positions, norm_attn_w, w_in, q_norm_w, k_norm_w, conv_w, conv_b, dt_bias, a_log, d_skip, ssd_norm_w, w_out, norm_ffn_w, router_group_w, router_group_b, router_expert_w, router_expert_b, w_gate, w_up, w_down):
    for layer in range(norm_attn_w.shape[0]):
        x = _layer(x, positions, norm_attn_w[layer], w_in[layer], q_norm_w[layer], k_norm_w[layer],
                   conv_w[layer], conv_b[layer], dt_bias[layer], a_log[layer], d_skip[layer],
                   ssd_norm_w[layer], w_out[layer], norm_ffn_w[layer], router_group_w[layer],
                   router_group_b[layer], router_expert_w[layer], router_expert_b[layer],
                   w_gate[layer], w_up[layer], w_down[layer])
    return x
```

```python
import functools

import jax
import jax.numpy as jnp
from jax import lax
from jax.experimental import pallas as pl
from jax.experimental.pallas import tpu as pltpu

F32 = jnp.float32
BF16 = jnp.bfloat16
I32 = jnp.int32
HIGHEST = lax.Precision.HIGHEST

HEAD_DIM = 128
DILATED_BRANCHES = ((128, 1), (512, 4), (2048, 16))
ATTN_BLOCK = 128
ROPE_THETA = 10000.0
SSD_HEAD_DIM = 64
SSD_GROUPS = 8
SSD_STATE = 128
SSD_CONV = 4
SSD_CHUNK = 256
N_EXPERT_GROUPS = 4
EXPERTS_PER_GROUP = 8
N_EXPERTS = N_EXPERT_GROUPS * EXPERTS_PER_GROUP
TOP_K = 2
NORM_EPS = 1e-6
SSD_NORM_EPS = 1e-5

LANES = 128
SUBLANES = 8
VMEM_LIMIT = 56 * 1024 * 1024

ROW_TILE = 512
IN_TM, IN_TN = 1024, 512
OUT_TM, OUT_TN = 1024, 512
MOE_ROWS = 1024
MOE_SUB = 128
ROW_UNROLL = 16
MOE_FF_TILE = 256
MOE_DOWN_SPLIT = 2
ROUTER_TILE = 256
ROUTER_EXPERT_LANE0 = N_EXPERT_GROUPS
ROUTER_PART_LANES = 40


def _silu(v):
    return v * (1.0 / (1.0 + jnp.exp(-v)))


def _nt_dot(a, b, **kw):
    return lax.dot_general(a, b, (((1,), (1,)), ((), ())), preferred_element_type=F32, **kw)


def _rmsnorm_cast_kernel(x_ref, w_ref, o_ref):
    x = x_ref[...]
    y = x * lax.rsqrt(jnp.mean(x * x, axis=-1, keepdims=True) + NORM_EPS)
    o_ref[...] = (y * w_ref[...]).astype(o_ref.dtype)


def _rmsnorm_cast(x2d, w):
    t, d = x2d.shape
    return pl.pallas_call(
        _rmsnorm_cast_kernel,
        grid=(t // ROW_TILE,),
        in_specs=[pl.BlockSpec((ROW_TILE, d), lambda i: (i, 0)),
                  pl.BlockSpec((1, d), lambda i: (0, 0))],
        out_specs=pl.BlockSpec((ROW_TILE, d), lambda i: (i, 0)),
        out_shape=jax.ShapeDtypeStruct((t, d), BF16),
        compiler_params=pltpu.CompilerParams(dimension_semantics=("parallel",), vmem_limit_bytes=VMEM_LIMIT),
        name="rmsnorm_cast",
    )(x2d, w.reshape(1, d))


def _matmul_nt_kernel(a_ref, wt_ref, o_ref):
    o_ref[...] = _nt_dot(a_ref[...], wt_ref[...].astype(BF16)).astype(o_ref.dtype)


def _matmul_nt(a, wt, n_rows, tm, tn, name):
    m, kdim = a.shape
    return pl.pallas_call(
        _matmul_nt_kernel,
        grid=(m // tm, n_rows // tn),
        in_specs=[pl.BlockSpec((tm, kdim), lambda i, j: (i, 0)),
                  pl.BlockSpec((tn, kdim), lambda i, j: (j, 0))],
        out_specs=pl.BlockSpec((tm, tn), lambda i, j: (i, j)),
        out_shape=jax.ShapeDtypeStruct((m, n_rows), F32),
        compiler_params=pltpu.CompilerParams(
            dimension_semantics=("parallel", "parallel"), vmem_limit_bytes=VMEM_LIMIT),
        name=name,
    )(a, wt)


def _out_proj_kernel(a1_ref, a2_ref, w_ref, x_ref, o_ref):
    k1 = a1_ref.shape[1]
    acc = jnp.dot(a1_ref[...], w_ref[0:k1, :].astype(BF16), preferred_element_type=F32)
    acc = acc + jnp.dot(a2_ref[...], w_ref[k1:, :].astype(BF16), preferred_element_type=F32)
    o_ref[...] = x_ref[...] + acc


def _out_proj(a1, a2, w, x2d, tm, tn):
    m, k1 = a1.shape
    k2 = a2.shape[1]
    kdim, n = w.shape
    return pl.pallas_call(
        _out_proj_kernel,
        grid=(m // tm, n // tn),
        in_specs=[pl.BlockSpec((tm, k1), lambda i, j: (i, 0)),
                  pl.BlockSpec((tm, k2), lambda i, j: (i, 0)),
                  pl.BlockSpec((kdim, tn), lambda i, j: (0, j)),
                  pl.BlockSpec((tm, tn), lambda i, j: (i, j))],
        out_specs=pl.BlockSpec((tm, tn), lambda i, j: (i, j)),
        out_shape=jax.ShapeDtypeStruct((m, n), F32),
        compiler_params=pltpu.CompilerParams(
            dimension_semantics=("parallel", "parallel"), vmem_limit_bytes=VMEM_LIMIT),
        name="out_proj",
    )(a1, a2, w, x2d)


def _attn_kernel(pos_ref, rope_ref, qw_ref, kw_ref, q_ref, k_ref, v_ref, o_ref,
                 cos_ref, sin_ref, rms_all_ref, qf_ref, kf_ref, qd_all_ref, kd_all_ref, vd_all_ref,
                 bias_all_ref, s_all_ref, p_all_ref, m_all_ref, o_br_ref, lse_br_ref, od_all_ref, lsed_all_ref):
    seq = q_ref.shape[0]
    blk = ATTN_BLOCK
    n_blocks = seq // blk

    @pl.when(pl.program_id(1) == 0)
    def _():
        ang = pos_ref[...].astype(F32) * rope_ref[0:1, :]
        cos_ref[...] = jnp.cos(ang)
        sin_ref[...] = jnp.sin(ang) * rope_ref[1:2, :]
        for bi, (window, dil) in enumerate(DILATED_BRANCHES):
            nb = seq // dil // blk
            kw = blk if nb == 1 else 2 * blk
            k_off = blk if nb == 1 else 0
            qi = lax.broadcasted_iota(I32, (blk, kw), 0)
            kj = lax.broadcasted_iota(I32, (blk, kw), 1)
            dist = (blk - k_off) + qi - kj
            ok = (dist >= 0) & (dist <= window // dil)
            bias_all_ref[bi, 0, :, 0:kw] = jnp.where(ok, 0.0, -jnp.inf)
            bias_all_ref[bi, 1, :, 0:kw] = jnp.where(ok & (kj >= blk - k_off), 0.0, -jnp.inf)

    def norm_rot(t_ref, w_ref, dst_ref, rms_ref, scale):
        def inv_rms(i, c):
            rows = pl.ds(pl.multiple_of(i * 256, 256), 256)
            t = t_ref[rows, :]
            r = lax.rsqrt(jnp.mean(t * t, axis=-1, keepdims=True) + NORM_EPS)
            rms_ref[rows, :] = jnp.broadcast_to(r, (256, HEAD_DIM))
            return c
        lax.fori_loop(0, seq // 256, inv_rms, 0, unroll=True)

        def rotate(i, c):
            rows = pl.ds(pl.multiple_of(i * 256, 256), 256)
            y = t_ref[rows, :] * rms_ref[rows, :] * w_ref[...]
            y = y * cos_ref[rows, :] + pltpu.roll(y, HEAD_DIM // 2, 1) * sin_ref[rows, :]
            dst_ref[rows, :] = y * scale if scale != 1.0 else y
            return c
        lax.fori_loop(0, seq // 256, rotate, 0, unroll=True)

    norm_rot(q_ref, qw_ref, qf_ref, rms_all_ref.at[0], HEAD_DIM ** -0.5)
    norm_rot(k_ref, kw_ref, kf_ref, rms_all_ref.at[1], 1.0)

    for bi, (window, dil) in enumerate(DILATED_BRANCHES):
        qd_ref, kd_ref, vd_ref = qd_all_ref.at[bi], kd_all_ref.at[bi], vd_all_ref.at[bi]
        bias_ref, s_ref, p_ref, m_ref = bias_all_ref.at[bi], s_all_ref.at[bi], p_all_ref.at[bi], m_all_ref.at[bi]
        od_ref, lsed_ref = od_all_ref.at[bi], lsed_all_ref.at[bi]
        kd_ref[0:blk, :] = jnp.zeros((blk, HEAD_DIM), BF16)
        vd_ref[0:blk, 0:HEAD_DIM] = jnp.zeros((blk, HEAD_DIM), BF16)
        vd_ref[:, HEAD_DIM:] = jnp.ones((blk + seq, HEAD_DIM), BF16)
        sub_len = seq // dil
        nb = sub_len // blk
        natural = dil == 1
        kw = blk if nb == 1 else 2 * blk
        k_off = blk if nb == 1 else 0
        for r in range(dil):
            src = slice(None) if natural else pl.ds(r, sub_len, stride=dil)
            dst = slice(r * sub_len, (r + 1) * sub_len)
            dstp = slice(blk + r * sub_len, blk + (r + 1) * sub_len)
            qd_ref[dst, :] = qf_ref[src, :].astype(BF16)
            kd_ref[dstp, :] = kf_ref[src, :].astype(BF16)
            vd_ref[dstp, 0:HEAD_DIM] = v_ref[src, :].astype(BF16)

        o_out = o_br_ref.at[bi] if natural else od_ref
        lse_out = lse_br_ref.at[bi] if natural else lsed_ref

        def scores(j, c):
            r0 = pl.multiple_of(j * blk, blk)
            s_ref[pl.ds(r0, blk), 0:kw] = _nt_dot(qd_ref[pl.ds(r0, blk), :], kd_ref[pl.ds(r0 + k_off, kw), :])
            return c
        lax.fori_loop(0, n_blocks, scores, 0, unroll=True)

        def softmax(j, c):
            r0 = pl.multiple_of(j * blk, blk)
            first = jnp.where((j & (nb - 1)) == 0, 1, 0)
            s = s_ref[pl.ds(r0, blk), 0:kw] + bias_ref[first, :, 0:kw]
            m = jnp.max(s, axis=1, keepdims=True)
            p_ref[pl.ds(r0, blk), 0:kw] = jnp.exp(s - m).astype(BF16)
            m_ref[pl.ds(r0, blk), :] = jnp.broadcast_to(m, (blk, HEAD_DIM))
            return c
        lax.fori_loop(0, n_blocks, softmax, 0, unroll=True)

        def weighted(j, c):
            r0 = pl.multiple_of(j * blk, blk)
            acc = jnp.dot(p_ref[pl.ds(r0, blk), 0:kw], vd_ref[pl.ds(r0 + k_off, kw), :], preferred_element_type=F32)
            l = acc[:, HEAD_DIM:]
            o_out[pl.ds(r0, blk), :] = acc[:, 0:HEAD_DIM] / l
            lse_out[pl.ds(r0, blk), :] = m_ref[pl.ds(r0, blk), :] + jnp.log(l)
            return c
        lax.fori_loop(0, n_blocks, weighted, 0, unroll=True)

        if not natural:
            for r in range(dil):
                src = slice(r * sub_len, (r + 1) * sub_len)
                dst = pl.ds(r, sub_len, stride=dil)
                o_br_ref[bi, dst, :] = od_ref[src, :]
                lse_br_ref[bi, dst, :] = lsed_ref[src, :]

    n_br = len(DILATED_BRANCHES)

    def merge(i, c):
        rows = pl.ds(pl.multiple_of(i * 256, 256), 256)
        lses = [lse_br_ref[b, rows, :] for b in range(n_br)]
        top = functools.reduce(jnp.maximum, lses)
        num = jnp.zeros((256, HEAD_DIM), F32)
        den = jnp.zeros((256, HEAD_DIM), F32)
        for b in range(n_br):
            e = jnp.exp(lses[b] - top)
            num = num + e * o_br_ref[b, rows, :]
            den = den + e
        o_ref[rows, :] = (num / den).astype(o_ref.dtype)
        return c
    lax.fori_loop(0, seq // 256, merge, 0, unroll=True)


def _attention(proj, pos_col, rope_tab, q_norm_w, k_norm_w, batch, seq, n_heads):
    t = batch * seq
    nbr = len(DILATED_BRANCHES)
    qkv_spec = lambda off: pl.BlockSpec((seq, HEAD_DIM), lambda b, h: (b, off + h))
    return pl.pallas_call(
        _attn_kernel,
        grid=(batch, n_heads),
        in_specs=[pl.BlockSpec((seq, 1), lambda b, h: (b, 0)),
                  pl.BlockSpec((2, HEAD_DIM), lambda b, h: (0, 0)),
                  pl.BlockSpec((1, HEAD_DIM), lambda b, h: (0, 0)),
                  pl.BlockSpec((1, HEAD_DIM), lambda b, h: (0, 0)),
                  qkv_spec(0), qkv_spec(n_heads), qkv_spec(2 * n_heads)],
        out_specs=pl.BlockSpec((seq, HEAD_DIM), lambda b, h: (b, h)),
        out_shape=jax.ShapeDtypeStruct((t, n_heads * HEAD_DIM), BF16),
        scratch_shapes=[
            pltpu.VMEM((seq, HEAD_DIM), F32),
            pltpu.VMEM((seq, HEAD_DIM), F32),
            pltpu.VMEM((2, seq, HEAD_DIM), F32),
            pltpu.VMEM((seq, HEAD_DIM), F32),
            pltpu.VMEM((seq, HEAD_DIM), F32),
            pltpu.VMEM((nbr, seq, HEAD_DIM), BF16),
            pltpu.VMEM((nbr, ATTN_BLOCK + seq, HEAD_DIM), BF16),
            pltpu.VMEM((nbr, ATTN_BLOCK + seq, 2 * HEAD_DIM), BF16),
            pltpu.VMEM((nbr, 2, ATTN_BLOCK, 2 * ATTN_BLOCK), F32),
            pltpu.VMEM((nbr, seq, 2 * ATTN_BLOCK), F32),
            pltpu.VMEM((nbr, seq, 2 * ATTN_BLOCK), BF16),
            pltpu.VMEM((nbr, seq, HEAD_DIM), F32),
            pltpu.VMEM((nbr, seq, HEAD_DIM), F32),
            pltpu.VMEM((nbr, seq, HEAD_DIM), F32),
            pltpu.VMEM((nbr, seq, HEAD_DIM), F32),
            pltpu.VMEM((nbr, seq, HEAD_DIM), F32),
        ],
        compiler_params=pltpu.CompilerParams(
            dimension_semantics=("parallel", "arbitrary"), vmem_limit_bytes=VMEM_LIMIT),
        name="dilated_attention",
    )(pos_col, rope_tab, q_norm_w.reshape(1, HEAD_DIM), k_norm_w.reshape(1, HEAD_DIM), proj, proj, proj)


def _ssd_kernel(z_ref, xbc_ref, dt_ref, convw_ref, convb_ref, dtb_ref, alog_ref, dskip_ref, normw_ref,
                o_ref, ext_ref, act_ref, state_ref, cs_ref, ecs_ref, cst_ref, wt_ref, dtt_ref, cdec_ref):
    lc = SSD_CHUNK
    width = o_ref.shape[1]
    gw = width // SSD_GROUPS
    heads_per_group = gw // SSD_HEAD_DIM
    conv_dim = xbc_ref.shape[1]
    halo = SUBLANES

    @pl.when(pl.program_id(1) == 0)
    def _():
        ext_ref[0:halo, :] = jnp.zeros((halo, conv_dim), F32)
        state_ref[...] = jnp.zeros_like(state_ref)

    ext_ref[halo:halo + lc, :] = xbc_ref[...]
    col_tile = 512
    for ct in range(conv_dim // col_tile):
        cols = slice(ct * col_tile, (ct + 1) * col_tile)
        acc = jnp.broadcast_to(convb_ref[:, cols], (lc, col_tile))
        for j in range(SSD_CONV):
            r0 = halo - (SSD_CONV - 1) + j
            acc = acc + convw_ref[j:j + 1, cols] * ext_ref[r0:r0 + lc, cols]
        act_ref[:, cols] = _silu(acc)
    ext_ref[0:halo, :] = xbc_ref[lc - halo:lc, :]

    x0 = dt_ref[...] + dtb_ref[...]
    dt = jnp.maximum(x0, 0.0) + jnp.log1p(jnp.exp(-jnp.abs(x0)))
    d_a = dt * (-jnp.exp(alog_ref[...]))
    tri = (lax.broadcasted_iota(I32, (lc, lc), 0) >= lax.broadcasted_iota(I32, (lc, lc), 1)).astype(F32)
    cs = jnp.dot(tri, d_a, precision=HIGHEST, preferred_element_type=F32)
    cs_last = cs[lc - 1:lc, :]
    cs_ref[...] = cs
    ecs_ref[...] = jnp.exp(cs)
    cdec_ref[...] = jnp.exp(cs_last)
    eye = (lax.broadcasted_iota(I32, (LANES, LANES), 0) == lax.broadcasted_iota(I32, (LANES, LANES), 1)).astype(F32)
    cst_ref[...] = _nt_dot(eye, cs, precision=HIGHEST)
    wt_ref[...] = _nt_dot(eye, jnp.exp(cs_last - cs) * dt, precision=HIGHEST)
    dtt_ref[...] = _nt_dot(eye, dt, precision=HIGHEST)

    causal = lax.broadcasted_iota(I32, (lc, lc), 0) >= lax.broadcasted_iota(I32, (lc, lc), 1)
    lane_head = lax.broadcasted_iota(I32, (lc, gw), 1) // SSD_HEAD_DIM
    lane_head_s = lax.broadcasted_iota(I32, (SSD_STATE, gw), 1) // SSD_HEAD_DIM
    n_xs = width
    n_b = SSD_GROUPS * SSD_STATE

    for g in range(SSD_GROUPS):
        gc = slice(g * gw, (g + 1) * gw)
        xs = act_ref[:, gc]
        b_g = act_ref[:, n_xs + g * SSD_STATE:n_xs + (g + 1) * SSD_STATE]
        c_g = act_ref[:, n_xs + n_b + g * SSD_STATE:n_xs + n_b + (g + 1) * SSD_STATE]
        xs_b = xs.astype(BF16)
        c_b = c_g.astype(BF16)
        cb = _nt_dot(c_b, b_g.astype(BF16))
        b_t = jnp.transpose(b_g)
        s_prev = state_ref[g]
        y_off = jnp.dot(c_b, s_prev.astype(BF16), preferred_element_type=F32)
        y_diag = ecs_g = s_in = cdec_g = None
        for r in range(heads_per_group):
            h = g * heads_per_group + r
            diff = cs_ref[:, h:h + 1] - cst_ref[h:h + 1, :]
            seg = jnp.exp(jnp.where(causal, diff, -jnp.inf))
            m_h = (cb * seg * dtt_ref[h:h + 1, :]).astype(BF16)
            y_h = jnp.dot(m_h, xs_b, preferred_element_type=F32)
            bw = (b_t * wt_ref[h:h + 1, :]).astype(BF16)
            s_h = jnp.dot(bw, xs_b, preferred_element_type=F32)
            ecs_h = jnp.broadcast_to(ecs_ref[:, h:h + 1], (lc, gw))
            cdec_h = jnp.broadcast_to(cdec_ref[:, h:h + 1], (1, gw))
            if r == 0:
                y_diag, ecs_g, s_in, cdec_g = y_h, ecs_h, s_h, cdec_h
            else:
                y_diag = jnp.where(lane_head == r, y_h, y_diag)
                ecs_g = jnp.where(lane_head == r, ecs_h, ecs_g)
                s_in = jnp.where(lane_head_s == r, s_h, s_in)
                cdec_g = jnp.where(lane_head_s[0:1, :] == r, cdec_h, cdec_g)
        state_ref[g] = cdec_g * s_prev + s_in
        y = y_diag + ecs_g * y_off
        y = y + dskip_ref[:, gc] * xs
        y = y * _silu(z_ref[:, gc])
        y = y * lax.rsqrt(jnp.mean(y * y, axis=-1, keepdims=True) + SSD_NORM_EPS)
        o_ref[:, gc] = (y * normw_ref[:, gc]).astype(o_ref.dtype)


def _ssd(proj, dt_raw, conv_w, conv_b, dt_bias_pad, a_log_pad, d_skip_ch, norm_w, batch, seq, z_col, xbc_col, width):
    t = batch * seq
    nc = seq // SSD_CHUNK
    conv_dim = conv_w.shape[1]
    gw = width // SSD_GROUPS
    row = lambda b, c: b * nc + c
    return pl.pallas_call(
        _ssd_kernel,
        grid=(batch, nc),
        in_specs=[pl.BlockSpec((SSD_CHUNK, width), lambda b, c: (row(b, c), z_col)),
                  pl.BlockSpec((SSD_CHUNK, conv_dim), lambda b, c: (row(b, c), xbc_col)),
                  pl.BlockSpec((SSD_CHUNK, LANES), lambda b, c: (row(b, c), 0)),
                  pl.BlockSpec((SSD_CONV, conv_dim), lambda b, c: (0, 0)),
                  pl.BlockSpec((1, conv_dim), lambda b, c: (0, 0)),
                  pl.BlockSpec((1, LANES), lambda b, c: (0, 0)),
                  pl.BlockSpec((1, LANES), lambda b, c: (0, 0)),
                  pl.BlockSpec((1, width), lambda b, c: (0, 0)),
                  pl.BlockSpec((1, width), lambda b, c: (0, 0))],
        out_specs=pl.BlockSpec((SSD_CHUNK, width), lambda b, c: (row(b, c), 0)),
        out_shape=jax.ShapeDtypeStruct((t, width), BF16),
        scratch_shapes=[
            pltpu.VMEM((SUBLANES + SSD_CHUNK, conv_dim), F32),
            pltpu.VMEM((SSD_CHUNK, conv_dim), F32),
            pltpu.VMEM((SSD_GROUPS, SSD_STATE, gw), F32),
            pltpu.VMEM((SSD_CHUNK, LANES), F32),
            pltpu.VMEM((SSD_CHUNK, LANES), F32),
            pltpu.VMEM((LANES, SSD_CHUNK), F32),
            pltpu.VMEM((LANES, SSD_CHUNK), F32),
            pltpu.VMEM((LANES, SSD_CHUNK), F32),
            pltpu.VMEM((1, LANES), F32),
        ],
        compiler_params=pltpu.CompilerParams(
            dimension_semantics=("parallel", "arbitrary"), vmem_limit_bytes=VMEM_LIMIT),
        name="ssd_mixer",
    )(proj, proj, dt_raw, conv_w, conv_b.reshape(1, conv_dim), dt_bias_pad, a_log_pad,
      d_skip_ch.reshape(1, width), norm_w.reshape(1, width))


def _router_kernel(x_ref, nw_ref, wr_ref, br_ref, h_ref, meta_ref, cnt_ref, carry_ref):
    tm = x_ref.shape[0]

    @pl.when(pl.program_id(0) == 0)
    def _():
        carry_ref[...] = jnp.zeros_like(carry_ref)

    x = x_ref[...]
    h = x * lax.rsqrt(jnp.mean(x * x, axis=-1, keepdims=True) + NORM_EPS) * nw_ref[...]
    h_ref[...] = h

    wp = wr_ref[...]
    h_hi = h.astype(BF16)
    rem = h - h_hi.astype(F32)
    h_mid = rem.astype(BF16)
    h_lo = (rem - h_mid.astype(F32)).astype(BF16)
    parts = (jnp.dot(h_hi, wp, preferred_element_type=F32) + jnp.dot(h_mid, wp, preferred_element_type=F32)
             + jnp.dot(h_lo, wp, preferred_element_type=F32))
    logits = parts
    for k in range(1, 3):
        logits = logits + pltpu.roll(parts, LANES - k * ROUTER_PART_LANES, 1)
    logits = logits + br_ref[...]
    lane = lax.broadcasted_iota(I32, (tm, LANES), 1)
    neg_inf = jnp.float32(-jnp.inf)

    lane_f = lane.astype(F32)

    def first_argmax(v, vmax):
        return jnp.min(jnp.where(v == vmax, lane_f, float(LANES)), axis=1, keepdims=True).astype(I32)

    lg = jnp.where(lane < N_EXPERT_GROUPS, logits, neg_inf)
    g_max = jnp.max(lg, axis=1, keepdims=True)
    g_idx = first_argmax(lg, g_max)
    g_p = 1.0 / jnp.sum(jnp.exp(lg - g_max), axis=1, keepdims=True)

    e_lo = ROUTER_EXPERT_LANE0 + EXPERTS_PER_GROUP * g_idx
    le = jnp.where((lane >= e_lo) & (lane < e_lo + EXPERTS_PER_GROUP), logits, neg_inf)
    e_max = jnp.max(le, axis=1, keepdims=True)
    i1 = first_argmax(le, e_max)
    e_sum = jnp.sum(jnp.exp(le - e_max), axis=1, keepdims=True)
    le2 = jnp.where(lane == i1, neg_inf, le)
    e_max2 = jnp.max(le2, axis=1, keepdims=True)
    i2 = first_argmax(le2, e_max2)
    p1 = 1.0 / e_sum
    p2 = jnp.exp(e_max2 - e_max) / e_sum
    gate1 = g_p * (p1 / (p1 + p2))
    gate2 = g_p * (p2 / (p1 + p2))

    hot = ((lane == i1) | (lane == i2))
    hot_b = hot.astype(BF16)
    strict = (lax.broadcasted_iota(I32, (tm, tm), 0) > lax.broadcasted_iota(I32, (tm, tm), 1)).astype(BF16)
    before = jnp.dot(strict, hot_b, preferred_element_type=F32) + carry_ref[0:1, :]
    rank1 = jnp.sum(jnp.where(lane == i1, before, 0.0), axis=1, keepdims=True)
    rank2 = jnp.sum(jnp.where(lane == i2, before, 0.0), axis=1, keepdims=True)
    carry_ref[0:1, :] = carry_ref[0:1, :] + jnp.sum(hot.astype(F32), axis=0, keepdims=True)
    cnt_ref[...] = jnp.broadcast_to(carry_ref[0:1, :], cnt_ref.shape)

    vals = [(i1 - ROUTER_EXPERT_LANE0).astype(F32), (i2 - ROUTER_EXPERT_LANE0).astype(F32), rank1, rank2, gate1, gate2]
    meta = jnp.zeros((tm, LANES), F32)
    for c, v in enumerate(vals):
        meta = jnp.where(lane == c, v, meta)
    meta_ref[...] = meta


def _router(x1, norm_w, w_router, b_router):
    t, d = x1.shape
    return pl.pallas_call(
        _router_kernel,
        grid=(t // ROUTER_TILE,),
        in_specs=[pl.BlockSpec((ROUTER_TILE, d), lambda i: (i, 0)),
                  pl.BlockSpec((1, d), lambda i: (0, 0)),
                  pl.BlockSpec((d, LANES), lambda i: (0, 0)),
                  pl.BlockSpec((1, LANES), lambda i: (0, 0))],
        out_specs=[pl.BlockSpec((ROUTER_TILE, d), lambda i: (i, 0)),
                   pl.BlockSpec((ROUTER_TILE, LANES), lambda i: (i, 0)),
                   pl.BlockSpec((SUBLANES, LANES), lambda i: (0, 0))],
        out_shape=[jax.ShapeDtypeStruct((t, d), F32),
                   jax.ShapeDtypeStruct((t, LANES), F32),
                   jax.ShapeDtypeStruct((SUBLANES, LANES), F32)],
        scratch_shapes=[pltpu.VMEM((SUBLANES, LANES), F32)],
        compiler_params=pltpu.CompilerParams(dimension_semantics=("arbitrary",), vmem_limit_bytes=VMEM_LIMIT),
        name="router",
    )(x1, norm_w.reshape(1, d), w_router, b_router)


def _wait_rows(src_ref, dst_ref, n, sem):
    done = 0
    for rows in (64, SUBLANES, 1):
        trips = lax.shift_right_logical(n - done, rows.bit_length() - 1)

        def body(r, c, rows=rows):
            pltpu.make_async_copy(src_ref.at[pl.ds(0, rows)], dst_ref.at[pl.ds(0, rows)], sem).wait()
            return c
        lax.fori_loop(0, trips, body, 0)
        done = done + trips * rows


def _for_rows(n, body):
    n_groups = lax.shift_right_logical(n, ROW_UNROLL.bit_length() - 1)

    def group(g, c):
        base = pl.multiple_of(g * ROW_UNROLL, ROW_UNROLL)
        for u in range(ROW_UNROLL):
            body(base + u, lambda buf, u=u: buf.at[pl.ds(base, ROW_UNROLL)].at[pl.ds(u, 1)])
        return c
    lax.fori_loop(0, n_groups, group, 0)

    def tail(r, c):
        body(r, lambda buf: buf.at[pl.ds(r, 1)])
        return c
    lax.fori_loop(n_groups * ROW_UNROLL, n, tail, 0)


def _for_row_blocks(n_rows, fn):
    n_blk = lax.shift_right_logical(n_rows + (MOE_SUB - 1), MOE_SUB.bit_length() - 1)
    for blocks in range(1, MOE_ROWS // MOE_SUB + 1):
        pl.when(n_blk == blocks)(functools.partial(fn, blocks * MOE_SUB))


def _moe_up_kernel(n_items_ref, item_e_ref, item_start_ref, item_n_ref, flat_ref,
                   h_hbm, wg_ref, wu_ref, o_ref, xbuf_ref, xb_ref, sem, *, n_tokens):
    i = pl.program_id(0)
    j = pl.program_id(1)
    n_items = n_items_ref[0]
    valid = i < n_items
    conv_rows = 64

    def start_gather(item):
        start = item_start_ref[item]

        def body(r, row_of):
            tok = flat_ref[start + r] & (n_tokens - 1)
            pltpu.make_async_copy(h_hbm.at[pl.ds(tok, 1)], row_of(xbuf_ref), sem).start()
        _for_rows(item_n_ref[item], body)

    @pl.when((i == 0) & (j == 0))
    def _():
        xbuf_ref[...] = jnp.zeros_like(xbuf_ref)
        start_gather(0)

    @pl.when(valid & (j == 0))
    def _():
        _wait_rows(h_hbm, xbuf_ref, item_n_ref[i], sem)

        def convert(s, c):
            rows = pl.ds(pl.multiple_of(s * conv_rows, conv_rows), conv_rows)
            xb_ref[rows, :] = xbuf_ref[rows, :].astype(BF16)
            return c
        lax.fori_loop(0, MOE_ROWS // conv_rows, convert, 0)

        @pl.when(i + 1 < n_items)
        def _():
            start_gather(i + 1)

    o_ref[...] = jnp.zeros_like(o_ref)

    def compute(m):
        xs = xb_ref[0:m, :]
        g = jnp.dot(xs, wg_ref[...].astype(BF16), preferred_element_type=F32)
        u = jnp.dot(xs, wu_ref[...].astype(BF16), preferred_element_type=F32)
        o_ref[0:m, :] = (_silu(g) * u).astype(o_ref.dtype)

    @pl.when(valid)
    def _():
        _for_row_blocks(item_n_ref[i], compute)


def _moe_up(sched, h2, w_gate, w_up, n_items_max):
    n_tokens, _ = h2.shape
    _, d, ff = w_gate.shape
    nj = ff // MOE_FF_TILE

    def w_map(i, j, n_items, item_e, *_):
        ii = jnp.clip(i, 0, jnp.maximum(n_items[0] - 1, 0))
        return (item_e[ii], 0, jnp.where(i < n_items[0], j, nj - 1))

    def o_map(i, j, *_):
        return (i, j)

    return pl.pallas_call(
        functools.partial(_moe_up_kernel, n_tokens=n_tokens),
        grid_spec=pltpu.PrefetchScalarGridSpec(
            num_scalar_prefetch=5,
            grid=(n_items_max, nj),
            in_specs=[pl.BlockSpec(memory_space=pl.ANY),
                      pl.BlockSpec((None, d, MOE_FF_TILE), w_map),
                      pl.BlockSpec((None, d, MOE_FF_TILE), w_map)],
            out_specs=pl.BlockSpec((MOE_ROWS, MOE_FF_TILE), o_map),
            scratch_shapes=[pltpu.VMEM((MOE_ROWS, d), F32),
                            pltpu.VMEM((MOE_ROWS, d), BF16),
                            pltpu.SemaphoreType.DMA(())]),
        out_shape=jax.ShapeDtypeStruct((n_items_max * MOE_ROWS, ff), BF16),
        compiler_params=pltpu.CompilerParams(
            dimension_semantics=("arbitrary", "arbitrary"), vmem_limit_bytes=VMEM_LIMIT),
        name="moe_up",
    )(*sched, h2, w_gate, w_up)


def _moe_down_kernel(n_items_ref, item_e_ref, item_start_ref, item_n_ref, flat_ref,
                     h_ref, wd_ref, y_hbm, *scratch):
    i = pl.program_id(0)
    j = pl.program_id(1)
    valid = i < n_items_ref[0]
    tn = wd_ref.shape[1]
    ybufs = scratch[:MOE_DOWN_SPLIT]
    sems = scratch[MOE_DOWN_SPLIT:2 * MOE_DOWN_SPLIT]
    pending_ref = scratch[2 * MOE_DOWN_SPLIT]

    @pl.when((i == 0) & (j == 0))
    def _():
        for jj in range(MOE_DOWN_SPLIT):
            pending_ref[jj] = 0

    def drain(jj):
        y_part = y_hbm.at[:, pl.ds(jj * tn, tn)]
        _wait_rows(ybufs[jj], y_part, pending_ref[jj], sems[jj])
        pending_ref[jj] = 0

    for jj in range(MOE_DOWN_SPLIT):
        @pl.when(valid & (j == jj))
        def _(jj=jj):
            ybuf = ybufs[jj]
            y_part = y_hbm.at[:, pl.ds(jj * tn, tn)]
            drain(jj)
            n = item_n_ref[i]
            start = item_start_ref[i]

            def compute(m):
                ybuf[0:m, :] = jnp.dot(h_ref[0:m, :], wd_ref[...].astype(BF16), preferred_element_type=F32)
            _for_row_blocks(n, compute)

            def issue(r, row_of):
                dest = flat_ref[start + r]
                pltpu.make_async_copy(row_of(ybuf), y_part.at[pl.ds(dest, 1)], sems[jj]).start()
            _for_rows(n, issue)
            pending_ref[jj] = n

    @pl.when((i == pl.num_programs(0) - 1) & (j == MOE_DOWN_SPLIT - 1))
    def _():
        for jj in range(MOE_DOWN_SPLIT):
            drain(jj)


def _moe_down(sched, h_items, w_down, n_items_max, n_tokens):
    _, ff, d = w_down.shape
    tn = d // MOE_DOWN_SPLIT

    def w_map(i, j, n_items, item_e, *_):
        ii = jnp.clip(i, 0, jnp.maximum(n_items[0] - 1, 0))
        return (item_e[ii], 0, jnp.where(i < n_items[0], j, MOE_DOWN_SPLIT - 1))

    def h_map(i, j, n_items, *_):
        return (jnp.clip(i, 0, jnp.maximum(n_items[0] - 1, 0)), 0)

    return pl.pallas_call(
        _moe_down_kernel,
        grid_spec=pltpu.PrefetchScalarGridSpec(
            num_scalar_prefetch=5,
            grid=(n_items_max, MOE_DOWN_SPLIT),
            in_specs=[pl.BlockSpec((MOE_ROWS, ff), h_map),
                      pl.BlockSpec((None, ff, tn), w_map)],
            out_specs=pl.BlockSpec(memory_space=pl.ANY),
            scratch_shapes=([pltpu.VMEM((MOE_ROWS, tn), F32)] * MOE_DOWN_SPLIT
                            + [pltpu.SemaphoreType.DMA(())] * MOE_DOWN_SPLIT
                            + [pltpu.SMEM((MOE_DOWN_SPLIT,), I32)])),
        out_shape=jax.ShapeDtypeStruct((TOP_K * n_tokens, d), F32),
        compiler_params=pltpu.CompilerParams(
            dimension_semantics=("arbitrary", "arbitrary"), vmem_limit_bytes=VMEM_LIMIT),
        name="moe_down",
    )(*sched, h_items, w_down)


def _combine_kernel(x_ref, meta_ref, y0_ref, y1_ref, o_ref):
    o_ref[...] = x_ref[...] + meta_ref[:, 4:5] * y0_ref[...] + meta_ref[:, 5:6] * y1_ref[...]


def _combine(x1, meta, y_slots):
    t, d = x1.shape
    tm = ROUTER_TILE
    nt = t // tm
    return pl.pallas_call(
        _combine_kernel,
        grid=(nt,),
        in_specs=[pl.BlockSpec((tm, d), lambda i: (i, 0)),
                  pl.BlockSpec((tm, LANES), lambda i: (i, 0)),
                  pl.BlockSpec((tm, d), lambda i: (i, 0)),
                  pl.BlockSpec((tm, d), lambda i: (nt + i, 0))],
        out_specs=pl.BlockSpec((tm, d), lambda i: (i, 0)),
        out_shape=jax.ShapeDtypeStruct((t, d), F32),
        compiler_params=pltpu.CompilerParams(dimension_semantics=("parallel",), vmem_limit_bytes=VMEM_LIMIT),
        name="moe_combine",
    )(x1, meta, y_slots, y_slots)


def _invert_permutation_kernel(pos_ref, inv_ref):
    def body(g, c):
        for u in range(ROW_UNROLL):
            a = g * ROW_UNROLL + u
            inv_ref[pos_ref[a]] = a
        return c
    lax.fori_loop(0, pos_ref.shape[0] // ROW_UNROLL, body, 0)


def _invert_permutation(pos):
    return pl.pallas_call(
        _invert_permutation_kernel,
        in_specs=[pl.BlockSpec(memory_space=pltpu.SMEM)],
        out_specs=pl.BlockSpec(memory_space=pltpu.SMEM),
        out_shape=jax.ShapeDtypeStruct(pos.shape, I32),
        name="invert_permutation",
    )(pos)


def _moe_schedule(meta, counts_f, n_tokens, n_items_max):
    ids = jnp.transpose(meta[:, 0:2 * TOP_K]).astype(I32)
    e_id = ids[0:TOP_K]
    rank = ids[TOP_K:2 * TOP_K]
    counts = counts_f[0, ROUTER_EXPERT_LANE0:ROUTER_EXPERT_LANE0 + N_EXPERTS].astype(I32)
    starts = jnp.cumsum(counts) - counts
    experts = jnp.arange(N_EXPERTS, dtype=I32)[:, None, None]
    start_of = jnp.sum(jnp.where(e_id[None] == experts, starts[:, None, None], 0), axis=0)
    pos = start_of + rank
    flat_sorted = _invert_permutation(pos.reshape(-1))
    chunks = (counts + MOE_ROWS - 1) // MOE_ROWS
    chunk_end = jnp.cumsum(chunks)
    n_items = chunk_end[-1]
    item = jnp.arange(n_items_max, dtype=I32)
    item_e = jnp.minimum(jnp.sum(item[:, None] >= chunk_end[None, :], axis=1), N_EXPERTS - 1).astype(I32)
    local = item - (chunk_end - chunks)[item_e]
    item_start = starts[item_e] + local * MOE_ROWS
    item_n = jnp.clip(counts[item_e] - local * MOE_ROWS, 0, MOE_ROWS)
    in_range = item < n_items
    item_start = jnp.where(in_range, item_start, 0).astype(I32)
    item_n = jnp.where(in_range, item_n, 0).astype(I32)
    return (n_items.reshape(1).astype(I32), item_e, item_start, item_n, flat_sorted)


def _layer(x, positions, norm_attn_w, w_in, q_norm_w, k_norm_w, conv_w, conv_b, dt_bias, a_log, d_skip,
           ssd_norm_w, w_out, norm_ffn_w, router_group_w, router_group_b, router_expert_w, router_expert_b,
           w_gate, w_up, w_down):
    batch, seq, d = x.shape
    t = batch * seq
    attn_width = d // 2
    n_heads = attn_width // HEAD_DIM
    ssd_width = d - attn_width
    ssd_heads = ssd_width // SSD_HEAD_DIM
    conv_dim = ssd_width + 2 * SSD_GROUPS * SSD_STATE
    main_cols = 3 * attn_width + ssd_width + conv_dim
    assert w_in.shape[1] == main_cols + ssd_heads and ssd_heads <= LANES
    assert seq % (SSD_CHUNK) == 0 and t % IN_TM == 0 and (t & (t - 1)) == 0

    x2d = x.reshape(t, d)
    h = _rmsnorm_cast(x2d, norm_attn_w)
    wt_in = jnp.transpose(w_in)
    proj = _matmul_nt(h, wt_in, main_cols, IN_TM, IN_TN, "in_proj")
    wt_dt = jnp.pad(wt_in[main_cols:, :], ((0, LANES - ssd_heads), (0, 0)))
    dt_raw = _matmul_nt(h, wt_dt, LANES, IN_TM, LANES, "dt_proj")

    half = HEAD_DIM // 2
    inv_freq = jnp.power(jnp.float32(ROPE_THETA), -jnp.arange(half, dtype=F32) / half)
    rope_tab = jnp.stack([jnp.concatenate([inv_freq, inv_freq]),
                          jnp.concatenate([-jnp.ones((half,), F32), jnp.ones((half,), F32)])])
    attn = _attention(proj, positions.reshape(t, 1), rope_tab, q_norm_w, k_norm_w, batch, seq, n_heads)

    pad_heads = lambda v: jnp.pad(v.astype(F32), (0, LANES - ssd_heads)).reshape(1, LANES)
    ssd = _ssd(proj, dt_raw, conv_w, conv_b, pad_heads(dt_bias), pad_heads(a_log),
               jnp.repeat(d_skip.astype(F32), SSD_HEAD_DIM), ssd_norm_w, batch, seq,
               z_col=(3 * attn_width) // ssd_width, xbc_col=(3 * attn_width + ssd_width) // conv_dim,
               width=ssd_width)

    x1 = _out_proj(attn, ssd, w_out, x2d, OUT_TM, OUT_TN)

    n_router = N_EXPERT_GROUPS + N_EXPERTS
    assert n_router <= ROUTER_PART_LANES and 3 * ROUTER_PART_LANES <= LANES
    w_cat = jnp.concatenate([router_group_w, router_expert_w], axis=1).astype(F32)
    w_parts, rem = [], w_cat
    for _ in range(3):
        part = rem.astype(BF16)
        w_parts.append(jnp.pad(part, ((0, 0), (0, ROUTER_PART_LANES - n_router))))
        rem = rem - part.astype(F32)
    w_router = jnp.pad(jnp.concatenate(w_parts, axis=1), ((0, 0), (0, LANES - 3 * ROUTER_PART_LANES)))
    b_router = jnp.pad(jnp.concatenate([router_group_b, router_expert_b]), (0, LANES - n_router)).reshape(1, LANES)
    h2, meta, counts_f = _router(x1, norm_ffn_w, w_router, b_router)

    n_items_max = N_EXPERTS + (TOP_K * t) // MOE_ROWS
    sched = _moe_schedule(meta, counts_f, t, n_items_max)
    h_items = _moe_up(sched, h2, w_gate, w_up, n_items_max)
    y_slots = _moe_down(sched, h_items, w_down, n_items_max, t)
    out = _combine(x1, meta, y_slots)
    return out.reshape(batch, seq, d)


def kernel(x, positions, norm_attn_w, w_in, q_norm_w, k_norm_w, conv_w, conv_b, dt_bias, a_log, d_skip, ssd_norm_w, w_out, norm_ffn_w, router_group_w, router_group_b, router_expert_w, router_expert_b, w_gate, w_up, w_down):
    for layer in range(norm_attn_w.shape[0]):
        x = _layer(x, positions, norm_attn_w[layer], w_in[layer], q_norm_w[layer], k_norm_w[layer],
                   conv_w[layer], conv_b[layer], dt_bias[layer], a_log[layer], d_skip[layer],
                   ssd_norm_w[layer], w_out[layer], norm_ffn_w[layer], router_group_w[layer],
                   router_group_b[layer], router_expert_w[layer], router_expert_b[layer],
                   w_gate[layer], w_up[layer], w_down[layer])
    return x
```
